```python
import jax, jax.numpy as jnp
from jax import lax
import numpy as np

D_MODEL = 1024
BATCH = 4
SEQ = 4096
DEPTH = 2
DEC_BATCH = 128
DEC_SEQ = 1
PAST_LEN = 16384
PAGE_SIZE = 128

N_A_LAYERS = DEPTH // 2
N_B_LAYERS = DEPTH - N_A_LAYERS
D_FF = 2816
HG_HEADS = 8
HG_DK = D_MODEL // HG_HEADS
HG_DV = D_MODEL // HG_HEADS
HG_CHUNK = 64
MLA_HEADS = 8
QK_NOPE = 128
QK_ROPE = 64
V_HEAD = 128
KV_RANK = 256
Q_RANK = 384
ROPE_THETA = 10000.0
Q_BLOCK = 128
EPS = 1e-6
SM_SCALE = (QK_NOPE + QK_ROPE) ** -0.5

kernel_name = 'hgrn2_mla_yoco_macaron_step'


def rmsnorm(x, g):
    xf = x.astype(jnp.float32)
    xf = xf * lax.rsqrt(jnp.mean(xf * xf, axis=-1, keepdims=True) + EPS)
    return xf.astype(x.dtype) * g


def rope(x, pos):
    half = QK_ROPE // 2
    inv_freq = ROPE_THETA ** (-2.0 * jnp.arange(half, dtype=jnp.float32) / QK_ROPE)
    ang = pos.astype(jnp.float32)[:, None] * inv_freq[None, :]
    ang = ang.reshape((ang.shape[0],) + (1,) * (x.ndim - 3) + (half,))
    cos, sin = jnp.cos(ang), jnp.sin(ang)
    xf = x.astype(jnp.float32)
    x1, x2 = xf[..., :half], xf[..., half:]
    return jnp.concatenate([x1 * cos - x2 * sin, x2 * cos + x1 * sin], axis=-1).astype(x.dtype)


def swiglu(h, w_gu, w_down):
    a, b = jnp.split(h @ w_gu, 2, axis=-1)
    return (jax.nn.silu(a) * b) @ w_down


def gla_chunked(q, k, v, logg):
    Bn, T, H, DK = q.shape
    DV = v.shape[-1]
    C = HG_CHUNK
    n = T // C

    def to_chunks(a):
        return a.reshape(Bn, n, C, H, a.shape[-1]).transpose(1, 0, 3, 2, 4)

    causal = jnp.tril(jnp.ones((C, C), dtype=bool))[:, :, None]

    def step(S, inp):
        qc, kc, vc, gc = inp
        b = jnp.cumsum(gc, axis=2)
        o_inter = jnp.einsum('bhtk,bhkv->bhtv', qc * jnp.exp(b), S)
        diff = b[:, :, :, None, :] - b[:, :, None, :, :]
        decay = jnp.exp(jnp.where(causal, diff, -jnp.inf))
        attn = jnp.einsum('bhtk,bhsk,bhtsk->bhts', qc, kc, decay)
        o = o_inter + jnp.einsum('bhts,bhsv->bhtv', attn, vc)
        b_last = b[:, :, -1:, :]
        S = jnp.exp(b_last[:, :, 0, :, None]) * S + jnp.einsum('bhsk,bhsv->bhkv', kc * jnp.exp(b_last - b), vc)
        return S, o

    S0 = jnp.zeros((Bn, H, DK, DV), jnp.float32)
    S, o = lax.scan(step, S0, (to_chunks(q), to_chunks(k), to_chunks(v), to_chunks(logg)))
    return o.transpose(1, 0, 3, 2, 4).reshape(Bn, T, H, DV), S


def gla_tokens(q, k, v, logg, S0):
    def step(S, inp):
        qt, kt, vt, gt = inp
        S = jnp.exp(gt)[..., None] * S + kt[..., :, None] * vt[..., None, :]
        return S, jnp.einsum('bhk,bhkv->bhv', qt, S)

    tm = lambda a: a.transpose(1, 0, 2, 3)
    S, o = lax.scan(step, S0, (tm(q), tm(k), tm(v), tm(logg)))
    return tm(o), S


def hgrn_layer(h, mix, a, lb, w_in, g_out, w_out):
    Bn, T, _ = h.shape
    q, f, i, g = jnp.split(h @ w_in, 4, axis=-1)
    ff = f.astype(jnp.float32)
    log_forget = jnp.logaddexp(jnp.log(lb), jnp.log1p(-lb) + jax.nn.log_sigmoid(ff))
    k = (1.0 - lb) * jax.nn.sigmoid(-ff)
    heads = lambda t: t.reshape(Bn, T, HG_HEADS, -1).astype(jnp.float32)
    o, S = mix(a, heads(jax.nn.silu(q)), heads(k), heads(i), heads(log_forget))
    o = rmsnorm(o, g_out.reshape(HG_HEADS, HG_DV)).reshape(Bn, T, -1).astype(h.dtype)
    return (o * jax.nn.silu(g)) @ w_out, S


def shared_kv(x, sc, pos, kv_w_ada, kv_b_ada, kv_g_norm, kv_w_down, kv_g_latent):
    shift, scale = jnp.split(sc @ kv_w_ada + kv_b_ada, 2, axis=-1)
    h = rmsnorm(x, kv_g_norm) * (1 + scale[:, None]) + shift[:, None]
    z = h @ kv_w_down
    return rmsnorm(z[..., :KV_RANK], kv_g_latent), rope(z[..., KV_RANK:], pos)


def attend_prompt(q_lat, q_rope, ckv, krope):
    Bn, S, H, R = q_lat.shape
    nb = S // Q_BLOCK

    def blocks(a):
        return a.reshape(Bn, nb, Q_BLOCK, H, a.shape[-1]).transpose(1, 0, 2, 3, 4)

    kpos = jnp.arange(S)

    def one_block(args):
        ql, qr, start = args
        s = jnp.einsum('bqhr,bkr->bhqk', ql, ckv) + jnp.einsum('bqhd,bkd->bhqk', qr, krope)
        qpos = start + jnp.arange(Q_BLOCK)
        s = jnp.where(kpos[None, :] <= qpos[:, None], s.astype(jnp.float32) * SM_SCALE, -jnp.inf)
        p = jax.nn.softmax(s, axis=-1).astype(ckv.dtype)
        return jnp.einsum('bhqk,bkr->bqhr', p, ckv)

    out = lax.map(one_block, (blocks(q_lat), blocks(q_rope), jnp.arange(nb) * Q_BLOCK))
    return out.transpose(1, 0, 2, 3, 4).reshape(Bn, S, H, R)


def attend_paged(q_lat, q_rope, ckv_new, krope_new, cache_ckv, cache_krope, page_table):
    T = q_lat.shape[1]
    R = ckv_new.shape[-1]
    causal = jnp.tril(jnp.ones((T, T), dtype=bool))[None]

    def one_seq(args):
        ql, qr, cn, kn, pages = args
        past_c = cache_ckv[pages].reshape(-1, R)
        past_k = cache_krope[pages].reshape(-1, QK_ROPE)
        s_past = jnp.einsum('thr,lr->htl', ql, past_c) + jnp.einsum('thd,ld->htl', qr, past_k)
        s_new = jnp.einsum('thr,ur->htu', ql, cn) + jnp.einsum('thd,ud->htu', qr, kn)
        s_new = jnp.where(causal, s_new.astype(jnp.float32) * SM_SCALE, -jnp.inf)
        s = jnp.concatenate([s_past.astype(jnp.float32) * SM_SCALE, s_new], axis=-1)
        p = jax.nn.softmax(s, axis=-1).astype(cn.dtype)
        L = past_c.shape[0]
        return jnp.einsum('htl,lr->thr', p[..., :L], past_c) + jnp.einsum('htu,ur->thr', p[..., L:], cn)

    return lax.map(one_seq, (q_lat, q_rope, ckv_new, krope_new, page_table))


def mla_layer(h, pos, attend, ckv, krope, w_dq, g_q, w_uq, w_uk, w_uv, w_out):
    Bn, T, _ = h.shape
    qc = rmsnorm(h @ w_dq, g_q)
    q = jnp.einsum('btr,rhd->bthd', qc, w_uq)
    q_lat = jnp.einsum('bthn,rhn->bthr', q[..., :QK_NOPE], w_uk)
    q_rope = rope(q[..., QK_NOPE:], pos)
    o_lat = attend(q_lat, q_rope, ckv, krope)
    o = jnp.einsum('bthr,rhv->bthv', o_lat, w_uv).reshape(Bn, T, -1)
    return o @ w_out


def trunk(x, c, pos, mix, attend, w_ada, b_ada, g_norm, w_ffn_gu, w_ffn_down, hg_w_in, hg_lb, hg_g_out,
          hg_w_out, kv_w_ada, kv_b_ada, kv_g_norm, kv_w_down, kv_g_latent, kv_w_uk, kv_w_uv, q_w_down,
          q_g_norm, q_w_up, attn_w_out, g_final):
    Bn, T, D = x.shape
    sc = jax.nn.silu(c)
    lb_all = jnp.cumsum(jax.nn.softmax(hg_lb.astype(jnp.float32), axis=0), axis=0)
    states = []
    ckv = krope = None
    for l in range(DEPTH):
        if l == N_A_LAYERS:
            ckv, krope = shared_kv(x, sc, pos, kv_w_ada, kv_b_ada, kv_g_norm, kv_w_down, kv_g_latent)
        mods = (sc @ w_ada[l] + b_ada[l]).reshape(Bn, 3, 3, D)[:, :, :, None, :]

        def mod(y, j):
            return rmsnorm(y, g_norm[l, j]) * (1 + mods[:, j, 1]) + mods[:, j, 0]

        x = x + 0.5 * mods[:, 0, 2] * swiglu(mod(x, 0), w_ffn_gu[l, 0], w_ffn_down[l, 0])
        h = mod(x, 1)
        if l < N_A_LAYERS:
            y, S = hgrn_layer(h, mix, l, lb_all[l], hg_w_in[l], hg_g_out[l], hg_w_out[l])
            states.append(S)
        else:
            b = l - N_A_LAYERS
            y = mla_layer(h, pos, attend, ckv, krope, q_w_down[b], q_g_norm[b], q_w_up[b], kv_w_uk, kv_w_uv,
                          attn_w_out[b])
        x = x + mods[:, 1, 2] * y
        x = x + 0.5 * mods[:, 2, 2] * swiglu(mod(x, 2), w_ffn_gu[l, 1], w_ffn_down[l, 1])
    return rmsnorm(x, g_final), jnp.stack(states), ckv, krope


def setup_inputs(seed: int = 0) -> dict:
    key = jax.random.key(seed)
    ks = iter(jax.random.split(key, 40))
    D = D_MODEL
    n_pages = PAST_LEN // PAGE_SIZE
    n_pool = (DEC_BATCH * n_pages * 5) // 4

    def nrm(shape, scale):
        return scale * jax.random.normal(next(ks), shape, jnp.float32)

    def gain(shape):
        return 1.0 + nrm(shape, 0.02)

    page_table = jax.random.permutation(next(ks), n_pool)[: DEC_BATCH * n_pages]
    page_table = page_table.reshape(DEC_BATCH, n_pages).astype(jnp.int32)
    return {
        'x_prompt': nrm((BATCH, SEQ, D), 1.0),
        'x_sample': nrm((DEC_BATCH, DEC_SEQ, D), 1.0),
        'c_prompt': nrm((BATCH, D), 1.0),
        'c_sample': nrm((DEC_BATCH, D), 1.0),
        'state_hgrn': nrm((N_A_LAYERS, DEC_BATCH, HG_HEADS, HG_DK, HG_DV), 0.5),
        'cache_ckv': nrm((n_pool, PAGE_SIZE, KV_RANK), 1.0),
        'cache_krope': nrm((n_pool, PAGE_SIZE, QK_ROPE), 1.0),
        'page_table': page_table,
        'w_ada': nrm((DEPTH, D, 9 * D), 0.5 * D ** -0.5),
        'b_ada': nrm((DEPTH, 9 * D), 0.02),
        'g_norm': gain((DEPTH, 3, D)),
        'w_ffn_gu': nrm((DEPTH, 2, D, 2 * D_FF), D ** -0.5),
        'w_ffn_down': nrm((DEPTH, 2, D_FF, D), D_FF ** -0.5),
        'hg_w_in': nrm((N_A_LAYERS, D, 4 * D), D ** -0.5),
        'hg_lb': nrm((N_A_LAYERS + 1, D), 0.5),
        'hg_g_out': gain((N_A_LAYERS, D)),
        'hg_w_out': nrm((N_A_LAYERS, D, D), D ** -0.5),
        'kv_w_ada': nrm((D, 2 * D), 0.5 * D ** -0.5),
        'kv_b_ada': nrm((2 * D,), 0.02),
        'kv_g_norm': gain((D,)),
        'kv_w_down': nrm((D, KV_RANK + QK_ROPE), D ** -0.5),
        'kv_g_latent': gain((KV_RANK,)),
        'kv_w_uk': nrm((KV_RANK, MLA_HEADS, QK_NOPE), KV_RANK ** -0.5),
        'kv_w_uv': nrm((KV_RANK, MLA_HEADS, V_HEAD), KV_RANK ** -0.5),
        'q_w_down': nrm((N_B_LAYERS, D, Q_RANK), D ** -0.5),
        'q_g_norm': gain((N_B_LAYERS, Q_RANK)),
        'q_w_up': nrm((N_B_LAYERS, Q_RANK, MLA_HEADS, QK_NOPE + QK_ROPE), Q_RANK ** -0.5),
        'attn_w_out': nrm((N_B_LAYERS, MLA_HEADS * V_HEAD, D), (MLA_HEADS * V_HEAD) ** -0.5),
        'g_final': gain((D,)),
    }


def reference(x_prompt, x_sample, c_prompt, c_sample, state_hgrn, cache_ckv, cache_krope, page_table, w_ada,
              b_ada, g_norm, w_ffn_gu, w_ffn_down, hg_w_in, hg_lb, hg_g_out, hg_w_out, kv_w_ada, kv_b_ada,
              kv_g_norm, kv_w_down, kv_g_latent, kv_w_uk, kv_w_uv, q_w_down, q_g_norm, q_w_up, attn_w_out,
              g_final):
    pos_prompt = jnp.arange(SEQ)
    pos_sample = PAST_LEN + jnp.arange(DEC_SEQ)

    def prompt_mix(a, q, k, v, g):
        return gla_chunked(q, k, v, g)

    def sample_mix(a, q, k, v, g):
        return gla_tokens(q, k, v, g, state_hgrn[a].astype(jnp.float32))

    def sample_attend(ql, qr, cn, kn):
        return attend_paged(ql, qr, cn, kn, cache_ckv, cache_krope, page_table)

    y_prompt, st_prompt, ckv_prompt, krope_prompt = trunk(
        x_prompt, c_prompt, pos_prompt, prompt_mix, attend_prompt, w_ada, b_ada, g_norm, w_ffn_gu, w_ffn_down,
        hg_w_in, hg_lb, hg_g_out, hg_w_out, kv_w_ada, kv_b_ada, kv_g_norm, kv_w_down, kv_g_latent, kv_w_uk,
        kv_w_uv, q_w_down, q_g_norm, q_w_up, attn_w_out, g_final)
    y_sample, st_sample, ckv_sample, krope_sample = trunk(
        x_sample, c_sample, pos_sample, sample_mix, sample_attend, w_ada, b_ada, g_norm, w_ffn_gu, w_ffn_down,
        hg_w_in, hg_lb, hg_g_out, hg_w_out, kv_w_ada, kv_b_ada, kv_g_norm, kv_w_down, kv_g_latent, kv_w_uk,
        kv_w_uv, q_w_down, q_g_norm, q_w_up, attn_w_out, g_final)
    return (y_prompt, y_sample, st_prompt.astype(state_hgrn.dtype), st_sample.astype(state_hgrn.dtype),
            ckv_prompt, krope_prompt, ckv_sample, krope_sample)
```

```python
import functools
from typing import NamedTuple

import jax
import jax.numpy as jnp
import numpy as np
from jax import lax
from jax.experimental import pallas as pl
from jax.experimental.pallas import tpu as pltpu

D_MODEL = 1024
DEPTH = 2
N_A_LAYERS = DEPTH // 2
PAST_LEN = 16384
PAGE_SIZE = 128
D_FF = 2816
HG_HEADS = 8
HG_DK = D_MODEL // HG_HEADS
HG_DV = D_MODEL // HG_HEADS
MLA_HEADS = 8
QK_NOPE = 128
QK_ROPE = 64
V_HEAD = 128
KV_RANK = 256
Q_RANK = 384
QK_LAT = KV_RANK + QK_ROPE
ROPE_THETA = 10000.0
EPS = 1e-6
SM_SCALE = (QK_NOPE + QK_ROPE) ** -0.5

BF16 = jnp.bfloat16
F32 = jnp.float32

V7X_VMEM_BYTES = 64 * 1024 * 1024
VMEM_LIMIT = V7X_VMEM_BYTES * 3 // 4

GLA_CHUNK = 64
GLA_SUB = 16
GLA_BLOCK = 512
GLA_SAFE_DECAY = 60.0

NT_DIMS = (((1,), (1,)), ((), ()))


class Group(NamedTuple):
    rows: int
    seq: int
    tm: int
    per_row: bool

    @property
    def tiles_per_seq(self):
        return self.seq // self.tm


def _params(*sem):
    return pltpu.CompilerParams(dimension_semantics=sem, vmem_limit_bytes=VMEM_LIMIT)


def _dot(a, b):
    return jnp.dot(a, b, preferred_element_type=F32)


def _dot_nt(a, b):
    return lax.dot_general(a, b, NT_DIMS, preferred_element_type=F32)


def _sigmoid(x):
    return 1.0 / (1.0 + jnp.exp(-x))


def _rms(x):
    return x * lax.rsqrt(jnp.mean(x * x, axis=-1, keepdims=True) + EPS)


def _mod_norm(x, g, scale, shift):
    return (_rms(x) * g) * (1.0 + scale) + shift


def _mod_spec(grp, mods, col):
    if grp.per_row:
        return pl.BlockSpec((grp.tm, D_MODEL), lambda *g: (g[0], col))
    tps = grp.tiles_per_seq
    return pl.BlockSpec((None, 1, D_MODEL), lambda *g: (g[0] // tps, 0, col))


def _pos_spec(grp, width):
    if grp.per_row:
        return pl.BlockSpec((grp.tm, width), lambda *g: (g[0], 0))
    tps = grp.tiles_per_seq
    return pl.BlockSpec((grp.tm, width), lambda *g: (g[0] % tps, 0))


def _row_spec(grp, width):
    return pl.BlockSpec((grp.tm, width), lambda *g: (g[0], 0))


def _full_spec(shape):
    nd = len(shape)
    return pl.BlockSpec(shape, lambda *g: (0,) * nd)


def _ada_body(c_ref, w_ref, b_ref, o_ref):
    c = c_ref[...]
    sc = (c * _sigmoid(c)).astype(BF16)
    o_ref[...] = _dot(sc, w_ref[...].astype(BF16)) + b_ref[...]


def _ada(c_all, w, b, layer, tn=1024):
    rows = c_all.shape[0]
    n = w.shape[-1]
    return pl.pallas_call(
        _ada_body,
        out_shape=jax.ShapeDtypeStruct((rows, n), F32),
        grid=(n // tn,),
        in_specs=[
            _full_spec((rows, D_MODEL)),
            pl.BlockSpec((None, D_MODEL, tn), lambda j: (layer, 0, j)),
            pl.BlockSpec((None, 1, tn), lambda j: (layer, 0, j)),
        ],
        out_specs=pl.BlockSpec((rows, tn), lambda j: (0, j)),
        compiler_params=_params("arbitrary"),
        name="ada_proj",
    )(c_all, w, b)


def _ffn_body(x_ref, shift_ref, scale_ref, gate_ref, g_ref, wg_ref, wu_ref, wd_ref, *rest, final_norm):
    if final_norm:
        gf_ref, o_ref, h_scr, acc_scr = rest
    else:
        o_ref, h_scr, acc_scr = rest
    j = pl.program_id(1)

    @pl.when(j == 0)
    def _():
        h = _mod_norm(x_ref[...], g_ref[...], scale_ref[...], shift_ref[...])
        h_scr[...] = h.astype(BF16)
        acc_scr[...] = jnp.zeros_like(acc_scr)

    h = h_scr[...]
    a = _dot(h, wg_ref[...])
    b = _dot(h, wu_ref[...])
    act = (a * _sigmoid(a)) * b
    acc_scr[...] += _dot(act.astype(BF16), wd_ref[...])

    @pl.when(j == pl.num_programs(1) - 1)
    def _():
        y = x_ref[...] + (0.5 * gate_ref[...]) * acc_scr[...]
        if final_norm:
            y = _rms(y) * gf_ref[...]
        o_ref[...] = y


def _ffn(grp, x, mods, sub, g_norm_row, w_gu, w_down, layer, half, g_final=None, tf=256):
    nf = D_FF // tf
    final_norm = g_final is not None
    in_specs = [
        _row_spec(grp, D_MODEL),
        _mod_spec(grp, mods, sub * 3 + 0),
        _mod_spec(grp, mods, sub * 3 + 1),
        _mod_spec(grp, mods, sub * 3 + 2),
        _full_spec((1, D_MODEL)),
        pl.BlockSpec((None, None, D_MODEL, tf), lambda i, j: (layer, half, 0, j)),
        pl.BlockSpec((None, None, D_MODEL, tf), lambda i, j: (layer, half, 0, j + nf)),
        pl.BlockSpec((None, None, tf, D_MODEL), lambda i, j: (layer, half, j, 0)),
    ]
    args = [x, mods, mods, mods, g_norm_row, w_gu, w_gu, w_down]
    if final_norm:
        in_specs.append(_full_spec((1, D_MODEL)))
        args.append(g_final)
    return pl.pallas_call(
        functools.partial(_ffn_body, final_norm=final_norm),
        out_shape=jax.ShapeDtypeStruct((grp.rows, D_MODEL), F32),
        grid=(grp.rows // grp.tm, nf),
        in_specs=in_specs,
        out_specs=pl.BlockSpec((grp.tm, D_MODEL), lambda i, j: (i, 0)),
        scratch_shapes=[pltpu.VMEM((grp.tm, D_MODEL), BF16), pltpu.VMEM((grp.tm, D_MODEL), F32)],
        compiler_params=_params("parallel", "arbitrary"),
        name="ffn_half",
    )(*args)


def _hgrn_in_body(x_ref, shift_ref, scale_ref, g_ref, w_ref, lb_ref, qs_ref, k_ref, v_ref, og_ref, lg_ref,
                  *, layer):
    d = D_MODEL
    h = _mod_norm(x_ref[...], g_ref[...], scale_ref[...], shift_ref[...]).astype(BF16)
    lbp = lb_ref[...]
    e = jnp.exp(lbp - jnp.max(lbp, axis=0, keepdims=True))
    lb = jnp.sum(e[: layer + 1], axis=0, keepdims=True) / jnp.sum(e, axis=0, keepdims=True)

    q = _dot(h, w_ref[:, 0:d])
    qs_ref[...] = (q * _sigmoid(q)).astype(qs_ref.dtype)
    f = _dot(h, w_ref[:, d:2 * d])
    log_sig = jnp.minimum(f, 0.0) - jnp.log(1.0 + jnp.exp(-jnp.abs(f)))
    a = jnp.log(lb)
    b = jnp.log(1.0 - lb) + log_sig
    lg_ref[...] = jnp.maximum(a, b) + jnp.log(1.0 + jnp.exp(-jnp.abs(a - b)))
    k_ref[...] = ((1.0 - lb) / (1.0 + jnp.exp(f))).astype(k_ref.dtype)
    v_ref[...] = _dot(h, w_ref[:, 2 * d:3 * d]).astype(v_ref.dtype)
    g = _dot(h, w_ref[:, 3 * d:4 * d])
    og_ref[...] = (g * _sigmoid(g)).astype(og_ref.dtype)


def _hgrn_in(grp, x, mods, g_norm_row, w_in, hg_lb, layer, stream_dtype):
    sub = 1
    outs = ([jax.ShapeDtypeStruct((grp.rows, D_MODEL), stream_dtype)] * 4
            + [jax.ShapeDtypeStruct((grp.rows, D_MODEL), F32)])
    return pl.pallas_call(
        functools.partial(_hgrn_in_body, layer=layer),
        out_shape=outs,
        grid=(grp.rows // grp.tm,),
        in_specs=[
            _row_spec(grp, D_MODEL),
            _mod_spec(grp, mods, sub * 3 + 0),
            _mod_spec(grp, mods, sub * 3 + 1),
            _full_spec((1, D_MODEL)),
            pl.BlockSpec((None, D_MODEL, 4 * D_MODEL), lambda i: (layer, 0, 0)),
            _full_spec(hg_lb.shape),
        ],
        out_specs=[_row_spec(grp, D_MODEL)] * 5,
        compiler_params=_params("parallel"),
        name="hgrn_in",
    )(x, mods, mods, g_norm_row, w_in, hg_lb)


def _segmented_cumsum(x, seg):
    row = lax.broadcasted_iota(jnp.int32, x.shape, 0) & (seg - 1)
    s = 1
    while s < seg:
        x = x + jnp.where(row >= s, pltpu.roll(x, s, 0), 0.0)
        s *= 2
    return x


def _gla_body(q_ref, k_ref, v_ref, og_ref, lg_ref, go_ref, y_ref, s_out_ref, st_scr, b_scr):
    c, sub = GLA_CHUNK, GLA_SUB
    t = pl.program_id(2)

    @pl.when(t == 0)
    def _():
        st_scr[...] = jnp.zeros_like(st_scr)

    b_scr[...] = _segmented_cumsum(lg_ref[...], c)
    tri = lax.broadcasted_iota(jnp.int32, (c, c), 1) <= lax.broadcasted_iota(jnp.int32, (c, c), 0)
    go = go_ref[...]

    def finish(r0, o):
        y = _rms(o) * go * og_ref[pl.ds(r0, c), :].astype(F32)
        y_ref[pl.ds(r0, c), :] = y.astype(y_ref.dtype)

    def chunk(ci, carry):
        r0 = pl.multiple_of(ci * c, c)
        b = b_scr[pl.ds(r0, c), :]
        b_last = b[c - 1:c, :]
        refs = [jnp.zeros((1, HG_DK), F32)] + [b[i * sub - 1:i * sub, :] for i in range(1, c // sub)]
        worst = functools.reduce(jnp.maximum, [refs[i] - b[(i + 1) * sub - 1:(i + 1) * sub, :]
                                               for i in range(c // sub)])
        safe = jnp.max(worst) <= GLA_SAFE_DECAY

        @pl.when(safe)
        def _():
            q = q_ref[pl.ds(r0, c), :].astype(F32)
            k = k_ref[pl.ds(r0, c), :].astype(F32)
            v = v_ref[pl.ds(r0, c), :]
            st = st_scr[...]
            o = _dot_nt((q * jnp.exp(b)).astype(BF16), st.astype(BF16))
            rows = []
            for i in range(c // sub):
                qi = (q[i * sub:(i + 1) * sub] * jnp.exp(b[i * sub:(i + 1) * sub] - refs[i])).astype(BF16)
                ki = (k * jnp.exp(jnp.minimum(refs[i] - b, GLA_SAFE_DECAY))).astype(BF16)
                rows.append(_dot_nt(qi, ki))
            att = jnp.where(tri, jnp.concatenate(rows, axis=0), 0.0)
            o = o + _dot(att.astype(BF16), v)
            kd = (k * jnp.exp(b_last - b)).astype(BF16)
            vt = v.astype(F32).T.astype(BF16)
            st_scr[...] = st * jnp.exp(b_last) + _dot(vt, kd)
            finish(r0, o)

        @pl.when(jnp.logical_not(safe))
        def _():
            lane = lax.broadcasted_iota(jnp.int32, (HG_DV, c), 1)
            sublane = lax.broadcasted_iota(jnp.int32, (c, HG_DK), 0)
            vt = v_ref[pl.ds(r0, c), :].astype(F32).T
            qf = q_ref[pl.ds(r0, c), :].astype(F32)
            kf = k_ref[pl.ds(r0, c), :].astype(F32)
            df = jnp.exp(lg_ref[pl.ds(r0, c), :])

            def token(ti, carry2):
                st, ot = carry2
                sel = lane == ti
                pick = lambda a: jnp.sum(jnp.where(sublane == ti, a, 0.0), axis=0, keepdims=True)
                v_col = jnp.sum(jnp.where(sel, vt, 0.0), axis=1, keepdims=True)
                st = st * pick(df) + v_col * pick(kf)
                o_col = jnp.sum(st * pick(qf), axis=1, keepdims=True)
                return st, jnp.where(sel, o_col, ot)

            st, ot = lax.fori_loop(0, c, token, (st_scr[...], jnp.zeros((HG_DV, c), F32)))
            st_scr[...] = st
            finish(r0, ot.T)

        return carry

    lax.fori_loop(0, q_ref.shape[0] // c, chunk, 0)

    @pl.when(t == pl.num_programs(2) - 1)
    def _():
        s_out_ref[...] = st_scr[...].T


def _gla_prompt(qs, k, v, og, lg, g_out_row, batch, seq):
    tb = GLA_BLOCK
    stream = pl.BlockSpec((None, tb, HG_DK), lambda b, h, t: (b, t, h))
    shp = (batch, seq, D_MODEL)
    return pl.pallas_call(
        _gla_body,
        out_shape=[jax.ShapeDtypeStruct(shp, BF16),
                   jax.ShapeDtypeStruct((batch, HG_HEADS, HG_DK, HG_DV), F32)],
        grid=(batch, HG_HEADS, seq // tb),
        in_specs=[stream, stream, stream, stream, stream,
                  pl.BlockSpec((1, HG_DV), lambda b, h, t: (0, h))],
        out_specs=[stream, pl.BlockSpec((None, None, HG_DK, HG_DV), lambda b, h, t: (b, h, 0, 0))],
        scratch_shapes=[pltpu.VMEM((HG_DV, HG_DK), F32), pltpu.VMEM((tb, HG_DK), F32)],
        compiler_params=_params("parallel", "parallel", "arbitrary"),
        name="gla_chunked",
    )(qs.reshape(shp), k.reshape(shp), v.reshape(shp), og.reshape(shp), lg.reshape(shp), g_out_row)


def _gla_step_body(q_ref, k_ref, v_ref, og_ref, lg_ref, go_ref, s_ref, y_ref, s_out_ref):
    go = go_ref[...]

    def one(bi, carry):
        qt = q_ref[bi].astype(F32).T
        kt = k_ref[bi].astype(F32).T
        dt = jnp.exp(lg_ref[bi]).T
        v = v_ref[bi].astype(F32)
        outs = []
        for h in range(HG_HEADS):
            s = s_ref[bi, h] * dt[:, h:h + 1] + kt[:, h:h + 1] * v[h:h + 1, :]
            s_out_ref[bi, h] = s
            outs.append(jnp.sum(s * qt[:, h:h + 1], axis=0, keepdims=True))
        o = jnp.concatenate(outs, axis=0)
        y_ref[bi] = (_rms(o) * go * og_ref[bi].astype(F32)).astype(y_ref.dtype)
        return carry

    lax.fori_loop(0, q_ref.shape[0], one, 0)


def _gla_step(qs, k, v, og, lg, g_out_heads, state, layer, bb=8):
    rows = qs.shape[0]
    shp = (rows, HG_HEADS, HG_DK)
    stream = pl.BlockSpec((bb, HG_HEADS, HG_DK), lambda i: (i, 0, 0))
    return pl.pallas_call(
        _gla_step_body,
        out_shape=[jax.ShapeDtypeStruct(shp, F32),
                   jax.ShapeDtypeStruct((rows, HG_HEADS, HG_DK, HG_DV), F32)],
        grid=(rows // bb,),
        in_specs=[stream, stream, stream, stream, stream, _full_spec((HG_HEADS, HG_DV)),
                  pl.BlockSpec((None, bb, HG_HEADS, HG_DK, HG_DV), lambda i: (layer, i, 0, 0, 0))],
        out_specs=[stream, pl.BlockSpec((bb, HG_HEADS, HG_DK, HG_DV), lambda i: (i, 0, 0, 0))],
        compiler_params=_params("parallel"),
        name="gla_step",
    )(qs.reshape(shp), k.reshape(shp), v.reshape(shp), og.reshape(shp), lg.reshape(shp), g_out_heads, state)


def _proj_res_body(a_ref, w_ref, x_ref, gate_ref, o_ref):
    o_ref[...] = x_ref[...] + gate_ref[...] * _dot(a_ref[...].astype(BF16), w_ref[...])


def _proj_res(grp, a, w, layer, x, mods, sub):
    kdim = a.shape[1]
    return pl.pallas_call(
        _proj_res_body,
        out_shape=jax.ShapeDtypeStruct((grp.rows, D_MODEL), F32),
        grid=(grp.rows // grp.tm,),
        in_specs=[_row_spec(grp, kdim),
                  pl.BlockSpec((None, kdim, D_MODEL), lambda i: (layer, 0, 0)),
                  _row_spec(grp, D_MODEL),
                  _mod_spec(grp, mods, sub * 3 + 2)],
        out_specs=_row_spec(grp, D_MODEL),
        compiler_params=_params("parallel"),
        name="proj_residual",
    )(a, w, x, mods)


def _kv_body(x_ref, shift_ref, scale_ref, g_ref, wc_ref, wr_ref, wrr_ref, gl_ref, cos_ref, sin_ref,
             ckv_ref, kr_ref, kcat_ref):
    h = _mod_norm(x_ref[...], g_ref[...], scale_ref[...], shift_ref[...]).astype(BF16)
    ckv = _rms(_dot(h, wc_ref[...])) * gl_ref[...]
    kr = _dot(h, wr_ref[...]) * cos_ref[...] + _dot(h, wrr_ref[...]) * sin_ref[...]
    ckv_ref[...] = ckv
    kr_ref[...] = kr
    kcat_ref[:, 0:KV_RANK] = ckv.astype(BF16)
    kcat_ref[:, KV_RANK:QK_LAT] = kr.astype(BF16)


def _shared_kv(grp, x, kvmods, g_norm_row, w_c, w_r, w_rr, g_latent_row, cos_t, sin_t):
    return pl.pallas_call(
        _kv_body,
        out_shape=[jax.ShapeDtypeStruct((grp.rows, KV_RANK), F32),
                   jax.ShapeDtypeStruct((grp.rows, QK_ROPE), F32),
                   jax.ShapeDtypeStruct((grp.rows, QK_LAT), BF16)],
        grid=(grp.rows // grp.tm,),
        in_specs=[_row_spec(grp, D_MODEL),
                  _mod_spec(grp, kvmods, 0),
                  _mod_spec(grp, kvmods, 1),
                  _full_spec((1, D_MODEL)),
                  _full_spec(w_c.shape), _full_spec(w_r.shape), _full_spec(w_rr.shape),
                  _full_spec((1, KV_RANK)),
                  _pos_spec(grp, QK_ROPE), _pos_spec(grp, QK_ROPE)],
        out_specs=[_row_spec(grp, KV_RANK), _row_spec(grp, QK_ROPE), _row_spec(grp, QK_LAT)],
        compiler_params=_params("parallel"),
        name="shared_kv",
    )(x, kvmods, kvmods, g_norm_row, w_c, w_r, w_rr, g_latent_row, cos_t, sin_t)


def _mla_q_body(x_ref, shift_ref, scale_ref, g_ref, wdq_ref, gq_ref, wn_ref, wr_ref, wrr_ref, wuk_ref,
                cos_ref, sin_ref, q_ref):
    h = _mod_norm(x_ref[...], g_ref[...], scale_ref[...], shift_ref[...]).astype(BF16)
    qc = (_rms(_dot(h, wdq_ref[...])) * gq_ref[...]).astype(BF16)
    qn = _dot(qc, wn_ref[...])
    qr = (_dot(qc, wr_ref[...]) * cos_ref[...] + _dot(qc, wrr_ref[...]) * sin_ref[...]) * SM_SCALE
    for hd in range(MLA_HEADS):
        ql = _dot(qn[:, hd * QK_NOPE:(hd + 1) * QK_NOPE].astype(BF16), wuk_ref[hd]) * SM_SCALE
        q_ref[hd, :, 0:KV_RANK] = ql.astype(q_ref.dtype)
        q_ref[hd, :, KV_RANK:QK_LAT] = qr[:, hd * QK_ROPE:(hd + 1) * QK_ROPE].astype(q_ref.dtype)


def _mla_q(grp, x, mods, g_norm_row, w_dq, g_q_row, w_n, w_r, w_rr, w_uk_t, cos_t, sin_t, layer_b):
    sub = 1
    nseq = grp.rows // grp.seq
    tps = grp.tiles_per_seq
    sel = lambda a: pl.BlockSpec((None,) + a.shape[1:], lambda i: (layer_b,) + (0,) * (a.ndim - 1))
    return pl.pallas_call(
        _mla_q_body,
        out_shape=jax.ShapeDtypeStruct((nseq, MLA_HEADS, grp.seq, QK_LAT), BF16),
        grid=(grp.rows // grp.tm,),
        in_specs=[_row_spec(grp, D_MODEL),
                  _mod_spec(grp, mods, sub * 3 + 0),
                  _mod_spec(grp, mods, sub * 3 + 1),
                  _full_spec((1, D_MODEL)),
                  sel(w_dq), sel(g_q_row), sel(w_n), sel(w_r), sel(w_rr),
                  _full_spec(w_uk_t.shape),
                  _pos_spec(grp, MLA_HEADS * QK_ROPE), _pos_spec(grp, MLA_HEADS * QK_ROPE)],
        out_specs=pl.BlockSpec((None, MLA_HEADS, grp.tm, QK_LAT), lambda i: (i // tps, 0, i % tps, 0)),
        compiler_params=_params("parallel"),
        name="mla_q",
    )(x, mods, mods, g_norm_row, w_dq, g_q_row, w_n, w_r, w_rr, w_uk_t, cos_t, sin_t)


def _attn_body(qi_ref, kj_ref, q_ref, k_ref, o_ref, m_scr, l_scr, acc_scr, *, tq):
    p = pl.program_id(1)
    i = qi_ref[p]
    j = kj_ref[p]
    rows = MLA_HEADS * tq

    @pl.when(j == 0)
    def _():
        m_scr[...] = jnp.full_like(m_scr, -jnp.inf)
        l_scr[...] = jnp.zeros_like(l_scr)
        acc_scr[...] = jnp.zeros_like(acc_scr)

    q = q_ref[...].reshape(rows, QK_LAT)
    k = k_ref[...]
    s = _dot_nt(q, k)

    def update(s):
        m_prev = m_scr[...]
        m_new = jnp.maximum(m_prev, jnp.max(s, axis=1, keepdims=True))
        alpha = jnp.exp(m_prev - m_new)
        pr = jnp.exp(s - m_new)
        l_scr[...] = alpha * l_scr[...] + jnp.sum(pr, axis=1, keepdims=True)
        acc_scr[...] = alpha * acc_scr[...] + _dot(pr.astype(BF16), k[:, 0:KV_RANK])
        m_scr[...] = m_new

    @pl.when(j < i)
    def _():
        update(s)

    @pl.when(j == i)
    def _():
        qpos = lax.broadcasted_iota(jnp.int32, s.shape, 0) & (tq - 1)
        kpos = lax.broadcasted_iota(jnp.int32, s.shape, 1)
        update(jnp.where(kpos <= qpos, s, -jnp.inf))
        out = acc_scr[...] / l_scr[...]
        for hd in range(MLA_HEADS):
            o_ref[:, hd * KV_RANK:(hd + 1) * KV_RANK] = out[hd * tq:(hd + 1) * tq].astype(o_ref.dtype)


def _attend_prompt(q, kcat, batch, seq, tq=256):
    nq = seq // tq
    pairs = [(i, j) for i in range(nq) for j in range(i + 1)]
    qi = jnp.asarray(np.array([p[0] for p in pairs], np.int32))
    kj = jnp.asarray(np.array([p[1] for p in pairs], np.int32))
    rows = MLA_HEADS * tq
    return pl.pallas_call(
        functools.partial(_attn_body, tq=tq),
        out_shape=jax.ShapeDtypeStruct((batch, seq, MLA_HEADS * KV_RANK), BF16),
        grid_spec=pltpu.PrefetchScalarGridSpec(
            num_scalar_prefetch=2,
            grid=(batch, len(pairs)),
            in_specs=[pl.BlockSpec((None, MLA_HEADS, tq, QK_LAT), lambda b, p, qi, kj: (b, 0, qi[p], 0)),
                      pl.BlockSpec((None, tq, QK_LAT), lambda b, p, qi, kj: (b, kj[p], 0))],
            out_specs=pl.BlockSpec((None, tq, MLA_HEADS * KV_RANK), lambda b, p, qi, kj: (b, qi[p], 0)),
            scratch_shapes=[pltpu.VMEM((rows, 1), F32), pltpu.VMEM((rows, 1), F32),
                            pltpu.VMEM((rows, KV_RANK), F32)],
        ),
        compiler_params=_params("parallel", "arbitrary"),
        name="attend_prompt",
    )(qi, kj, q, kcat.reshape(batch, seq, QK_LAT))


PAGED_CHUNK_PAGES = 16


def _paged_body(pt_ref, q_ref, cn_ref, kn_ref, ckv_hbm, kr_hbm, o_ref, cbuf, kbuf, sems, *, n_pages):
    ch = PAGED_CHUNK_PAGES
    nc = n_pages // ch
    nseq = pl.num_programs(0)
    b = pl.program_id(0)

    def copies(g, slot):
        out = []
        for pg in range(ch):
            page = pt_ref[g * ch + pg]
            dst = pl.ds(pg * PAGE_SIZE, PAGE_SIZE)
            out.append(pltpu.make_async_copy(ckv_hbm.at[page], cbuf.at[slot, dst, :], sems.at[0, slot]))
            out.append(pltpu.make_async_copy(kr_hbm.at[page], kbuf.at[slot, dst, :], sems.at[1, slot]))
        return out

    def start(g, slot):
        for cp in copies(g, slot):
            cp.start()

    def wait(slot):
        for cp in copies(0, slot):
            cp.wait()

    @pl.when(b == 0)
    def _():
        start(0, 0)

    q = q_ref[...]
    ql = q[:, 0:KV_RANK]
    qr = q[:, KV_RANK:QK_LAT]
    cn = cn_ref[...]
    kn = kn_ref[...]
    m = jnp.sum(ql * cn, axis=1, keepdims=True) + jnp.sum(qr * kn, axis=1, keepdims=True)
    l = jnp.ones_like(m)
    acc = jnp.broadcast_to(cn, (MLA_HEADS, KV_RANK))
    qlb = ql.astype(BF16)
    qrb = qr.astype(BF16)
    for c in range(nc):
        slot = c % 2
        g = b * nc + c
        if c + 1 < nc:
            start(g + 1, 1 - slot)
        else:
            @pl.when(b + 1 < nseq)
            def _():
                start(g + 1, 1 - slot)
        wait(slot)
        kc = cbuf[slot].astype(BF16)
        kr = kbuf[slot].astype(BF16)
        s = _dot_nt(qlb, kc) + _dot_nt(qrb, kr)
        m_new = jnp.maximum(m, jnp.max(s, axis=1, keepdims=True))
        alpha = jnp.exp(m - m_new)
        pr = jnp.exp(s - m_new)
        l = alpha * l + jnp.sum(pr, axis=1, keepdims=True)
        acc = alpha * acc + _dot(pr.astype(BF16), kc)
        m = m_new
    o_ref[...] = acc / l


def _attend_paged(q, ckv_new, krope_new, cache_ckv, cache_krope, page_table):
    nseq, n_pages = page_table.shape
    ch = PAGED_CHUNK_PAGES
    assert n_pages % (2 * ch) == 0
    per_seq = lambda *tail: pl.BlockSpec((None,) + tail, lambda i, pt: (i,) + (0,) * len(tail))
    return pl.pallas_call(
        functools.partial(_paged_body, n_pages=n_pages),
        out_shape=jax.ShapeDtypeStruct((nseq, MLA_HEADS, KV_RANK), F32),
        grid_spec=pltpu.PrefetchScalarGridSpec(
            num_scalar_prefetch=1,
            grid=(nseq,),
            in_specs=[per_seq(MLA_HEADS, QK_LAT), per_seq(1, KV_RANK), per_seq(1, QK_ROPE),
                      pl.BlockSpec(memory_space=pl.ANY), pl.BlockSpec(memory_space=pl.ANY)],
            out_specs=per_seq(MLA_HEADS, KV_RANK),
            scratch_shapes=[pltpu.VMEM((2, ch * PAGE_SIZE, KV_RANK), F32),
                            pltpu.VMEM((2, ch * PAGE_SIZE, QK_ROPE), F32),
                            pltpu.SemaphoreType.DMA((2, 2))],
        ),
        compiler_params=_params("arbitrary"),
        name="attend_paged",
    )(page_table.reshape(-1), q, ckv_new.reshape(nseq, 1, KV_RANK), krope_new.reshape(nseq, 1, QK_ROPE),
      cache_ckv, cache_krope)


def _attn_out_body(o_ref, wuv_ref, wo_ref, x_ref, gate_ref, out_ref):
    parts = [_dot(o_ref[:, hd * KV_RANK:(hd + 1) * KV_RANK], wuv_ref[hd]).astype(BF16)
             for hd in range(MLA_HEADS)]
    o = jnp.concatenate(parts, axis=1)
    out_ref[...] = x_ref[...] + gate_ref[...] * _dot(o, wo_ref[...])


def _attn_out(grp, o_lat, w_uv_t, w_out, layer_b, x, mods, sub):
    width = MLA_HEADS * KV_RANK
    return pl.pallas_call(
        _attn_out_body,
        out_shape=jax.ShapeDtypeStruct((grp.rows, D_MODEL), F32),
        grid=(grp.rows // grp.tm,),
        in_specs=[_row_spec(grp, width),
                  _full_spec(w_uv_t.shape),
                  pl.BlockSpec((None, MLA_HEADS * V_HEAD, D_MODEL), lambda i: (layer_b, 0, 0)),
                  _row_spec(grp, D_MODEL),
                  _mod_spec(grp, mods, sub * 3 + 2)],
        out_specs=_row_spec(grp, D_MODEL),
        compiler_params=_params("parallel"),
        name="attn_out",
    )(o_lat, w_uv_t, w_out, x, mods)


def _rope_tables(pos):
    half = QK_ROPE // 2
    inv_freq = ROPE_THETA ** (-2.0 * jnp.arange(half, dtype=F32) / QK_ROPE)
    ang = pos.astype(F32)[:, None] * inv_freq[None, :]
    cos, sin = jnp.cos(ang), jnp.sin(ang)
    return jnp.concatenate([cos, cos], axis=-1), jnp.concatenate([-sin, sin], axis=-1)


def _swap_halves(w):
    half = QK_ROPE // 2
    return jnp.concatenate([w[..., half:], w[..., :half]], axis=-1)


def kernel(x_prompt, x_sample, c_prompt, c_sample, state_hgrn, cache_ckv, cache_krope, page_table, w_ada, b_ada, g_norm, w_ffn_gu, w_ffn_down, hg_w_in, hg_lb, hg_g_out, hg_w_out, kv_w_ada, kv_b_ada, kv_g_norm, kv_w_down, kv_g_latent, kv_w_uk, kv_w_uv, q_w_down, q_g_norm, q_w_up, attn_w_out, g_final):
    batch, seq, d = x_prompt.shape
    nseq = x_sample.shape[0]
    n_b = DEPTH - N_A_LAYERS

    w_gu = w_ffn_gu.astype(BF16)
    w_dn = w_ffn_down.astype(BF16)
    w_in = hg_w_in.astype(BF16)
    w_ho = hg_w_out.astype(BF16)
    w_ao = attn_w_out.astype(BF16)
    w_kc = kv_w_down[:, :KV_RANK].astype(BF16)
    w_kr = kv_w_down[:, KV_RANK:].astype(BF16)
    w_krr = _swap_halves(kv_w_down[:, KV_RANK:]).astype(BF16)
    w_dq = q_w_down.astype(BF16)
    w_qn = q_w_up[..., :QK_NOPE].reshape(n_b, Q_RANK, MLA_HEADS * QK_NOPE).astype(BF16)
    w_qr = q_w_up[..., QK_NOPE:].reshape(n_b, Q_RANK, MLA_HEADS * QK_ROPE).astype(BF16)
    w_qrr = _swap_halves(q_w_up[..., QK_NOPE:]).reshape(n_b, Q_RANK, MLA_HEADS * QK_ROPE).astype(BF16)
    w_uk_t = kv_w_uk.transpose(1, 2, 0).astype(BF16)
    w_uv_t = kv_w_uv.transpose(1, 0, 2).astype(BF16)
    g_q_rows = q_g_norm.reshape(n_b, 1, Q_RANK)
    g_out_rows = hg_g_out.reshape(N_A_LAYERS, 1, D_MODEL)

    pad = 8
    c_all = jnp.concatenate([c_prompt, jnp.zeros((pad - batch, d), F32), c_sample], axis=0)
    mods_all = [_ada(c_all, w_ada, b_ada.reshape(DEPTH, 1, -1), l) for l in range(DEPTH)]
    kvm_all = _ada(c_all, kv_w_ada[None], kv_b_ada.reshape(1, 1, -1), 0)

    def trunk(grp, x, mods_l, kvmods, pos, mix, attend, stream_dtype):
        grp_half = grp._replace(tm=max(grp.tm // 2, 128))
        cos_k, sin_k = _rope_tables(pos)
        cos_q, sin_q = jnp.tile(cos_k, (1, MLA_HEADS)), jnp.tile(sin_k, (1, MLA_HEADS))
        states = []
        ckv = krope = kcat = None
        for l in range(DEPTH):
            mods = mods_l[l]
            gn = lambda j: g_norm[l, j].reshape(1, d)
            if l == N_A_LAYERS:
                ckv, krope, kcat = _shared_kv(grp, x, kvmods, kv_g_norm.reshape(1, d), w_kc, w_kr, w_krr,
                                              kv_g_latent.reshape(1, KV_RANK), cos_k, sin_k)
            x = _ffn(grp, x, mods, 0, gn(0), w_gu, w_dn, l, 0)
            if l < N_A_LAYERS:
                qs, k, v, og, lg = _hgrn_in(grp_half, x, mods, gn(1), w_in, hg_lb, l, stream_dtype)
                y, s = mix(l, qs, k, v, og, lg)
                states.append(s)
                x = _proj_res(grp, y, w_ho, l, x, mods, 1)
            else:
                lb_ = l - N_A_LAYERS
                q = _mla_q(grp_half, x, mods, gn(1), w_dq, g_q_rows, w_qn, w_qr, w_qrr, w_uk_t, cos_q, sin_q,
                           lb_)
                o_lat = attend(q, ckv, krope, kcat)
                x = _attn_out(grp, o_lat, w_uv_t, w_ao, lb_, x, mods, 1)
            x = _ffn(grp, x, mods, 2, gn(2), w_gu, w_dn, l, 1,
                     g_final=g_final.reshape(1, d) if l == DEPTH - 1 else None)
        return x, jnp.stack(states), ckv, krope

    grp_p = Group(rows=batch * seq, seq=seq, tm=1024, per_row=False)
    mods_p = [m[:pad].reshape(pad, 1, -1) for m in mods_all]
    kvm_p = kvm_all[:pad].reshape(pad, 1, -1)

    def mix_p(l, qs, k, v, og, lg):
        y, s = _gla_prompt(qs, k, v, og, lg, g_out_rows[l], batch, seq)
        return y.reshape(batch * seq, d), s

    def attend_p(q, ckv, krope, kcat):
        return _attend_prompt(q, kcat, batch, seq).reshape(batch * seq, MLA_HEADS * KV_RANK)

    y_p, st_p, ckv_p, kr_p = trunk(grp_p, x_prompt.reshape(batch * seq, d), mods_p, kvm_p,
                                   jnp.arange(seq), mix_p, attend_p, BF16)

    grp_s = Group(rows=nseq, seq=nseq, tm=nseq, per_row=True)
    mods_s = [m[pad:] for m in mods_all]
    kvm_s = kvm_all[pad:]

    def mix_s(l, qs, k, v, og, lg):
        y, s = _gla_step(qs, k, v, og, lg, hg_g_out[l].reshape(HG_HEADS, HG_DV), state_hgrn, l)
        return y.reshape(nseq, d), s

    def attend_s(q, ckv, krope, kcat):
        qf = q[0].transpose(1, 0, 2).astype(F32)
        o = _attend_paged(qf, ckv, krope, cache_ckv, cache_krope, page_table)
        return o.reshape(nseq, MLA_HEADS * KV_RANK).astype(BF16)

    y_s, st_s, ckv_s, kr_s = trunk(grp_s, x_sample.reshape(nseq, d), mods_s, kvm_s,
                                   jnp.full((nseq,), PAST_LEN), mix_s, attend_s, F32)

    return (y_p.reshape(batch, seq, d), y_s.reshape(nseq, 1, d),
            st_p.astype(state_hgrn.dtype), st_s.astype(state_hgrn.dtype),
            ckv_p.reshape(batch, seq, KV_RANK), kr_p.reshape(batch, seq, QK_ROPE),
            ckv_s.reshape(nseq, 1, KV_RANK), kr_s.reshape(nseq, 1, QK_ROPE))
```

```python
import functools
from typing import NamedTuple

import jax
import jax.numpy as jnp
import numpy as np
from jax import lax
from jax.experimental import pallas as pl
from jax.experimental.pallas import tpu as pltpu

D_MODEL = 1024
DEPTH = 2
N_A_LAYERS = DEPTH // 2
PAST_LEN = 16384
PAGE_SIZE = 128
D_FF = 2816
HG_HEADS = 8
HG_DK = D_MODEL // HG_HEADS
HG_DV = D_MODEL // HG_HEADS
MLA_HEADS = 8
QK_NOPE = 128
QK_ROPE = 64
V_HEAD = 128
KV_RANK = 256
Q_RANK = 384
QK_LAT = KV_RANK + QK_ROPE
ROPE_THETA = 10000.0
EPS = 1e-6
SM_SCALE = (QK_NOPE + QK_ROPE) ** -0.5

BF16 = jnp.bfloat16
F32 = jnp.float32

V7X_VMEM_BYTES = 64 * 1024 * 1024
VMEM_LIMIT = V7X_VMEM_BYTES * 3 // 4

GLA_CHUNK = 64
GLA_SUB = 16
GLA_BLOCK = 256
GLA_SAFE_DECAY = 60.0

ATTN_TQ = 256

NT_DIMS = (((1,), (1,)), ((), ()))


class Group(NamedTuple):
    rows: int
    seq: int
    tm: int
    per_row: bool

    @property
    def tiles_per_seq(self):
        return self.seq // self.tm


def _params(*sem):
    return pltpu.CompilerParams(dimension_semantics=sem, vmem_limit_bytes=VMEM_LIMIT)


def _dot(a, b):
    return jnp.dot(a, b, preferred_element_type=F32)


def _dot_nt(a, b):
    return lax.dot_general(a, b, NT_DIMS, preferred_element_type=F32)


def _sigmoid(x):
    return 1.0 / (1.0 + jnp.exp(-x))


def _rms(x):
    return x * lax.rsqrt(jnp.mean(x * x, axis=-1, keepdims=True) + EPS)


def _mod_norm(x, g, scale, shift):
    return (_rms(x) * g) * (1.0 + scale) + shift


def _mod_spec(grp, mods, col):
    if grp.per_row:
        return pl.BlockSpec((grp.tm, D_MODEL), lambda *g: (g[0], col))
    tps = grp.tiles_per_seq
    return pl.BlockSpec((None, 1, D_MODEL), lambda *g: (g[0] // tps, 0, col))


def _pos_spec(grp, width):
    if grp.per_row:
        return pl.BlockSpec((grp.tm, width), lambda *g: (g[0], 0))
    tps = grp.tiles_per_seq
    return pl.BlockSpec((grp.tm, width), lambda *g: (g[0] % tps, 0))


def _row_spec(grp, width):
    return pl.BlockSpec((grp.tm, width), lambda *g: (g[0], 0))


def _full_spec(shape):
    nd = len(shape)
    return pl.BlockSpec(shape, lambda *g: (0,) * nd)


def _ada_body(c_ref, w_ref, b_ref, o_ref):
    c = c_ref[...]
    sc = (c * _sigmoid(c)).astype(BF16)
    o_ref[...] = _dot(sc, w_ref[...].astype(BF16)) + b_ref[...]


def _ada(c_all, w, b, layer, tn=1024):
    rows = c_all.shape[0]
    n = w.shape[-1]
    return pl.pallas_call(
        _ada_body,
        out_shape=jax.ShapeDtypeStruct((rows, n), F32),
        grid=(n // tn,),
        in_specs=[
            _full_spec((rows, D_MODEL)),
            pl.BlockSpec((None, D_MODEL, tn), lambda j: (layer, 0, j)),
            pl.BlockSpec((None, 1, tn), lambda j: (layer, 0, j)),
        ],
        out_specs=pl.BlockSpec((rows, tn), lambda j: (0, j)),
        compiler_params=_params("arbitrary"),
        name="ada_proj",
    )(c_all, w, b)


def _ffn_body(x_ref, shift_ref, scale_ref, gate_ref, g_ref, wg_ref, wu_ref, wd_ref, *rest, final_norm):
    if final_norm:
        gf_ref, o_ref, h_scr, acc_scr = rest
    else:
        o_ref, h_scr, acc_scr = rest
    j = pl.program_id(1)

    @pl.when(j == 0)
    def _():
        h = _mod_norm(x_ref[...], g_ref[...], scale_ref[...], shift_ref[...])
        h_scr[...] = h.astype(BF16)
        acc_scr[...] = jnp.zeros_like(acc_scr)

    h = h_scr[...]
    a = _dot(h, wg_ref[...])
    b = _dot(h, wu_ref[...])
    act = (a * _sigmoid(a)) * b
    acc_scr[...] += _dot(act.astype(BF16), wd_ref[...])

    @pl.when(j == pl.num_programs(1) - 1)
    def _():
        y = x_ref[...] + (0.5 * gate_ref[...]) * acc_scr[...]
        if final_norm:
            y = _rms(y) * gf_ref[...]
        o_ref[...] = y


def _ffn(grp, x, mods, sub, g_norm_row, w_gu, w_down, layer, half, g_final=None, tf=256):
    nf = D_FF // tf
    final_norm = g_final is not None
    in_specs = [
        _row_spec(grp, D_MODEL),
        _mod_spec(grp, mods, sub * 3 + 0),
        _mod_spec(grp, mods, sub * 3 + 1),
        _mod_spec(grp, mods, sub * 3 + 2),
        _full_spec((1, D_MODEL)),
        pl.BlockSpec((None, None, D_MODEL, tf), lambda i, j: (layer, half, 0, j)),
        pl.BlockSpec((None, None, D_MODEL, tf), lambda i, j: (layer, half, 0, j + nf)),
        pl.BlockSpec((None, None, tf, D_MODEL), lambda i, j: (layer, half, j, 0)),
    ]
    args = [x, mods, mods, mods, g_norm_row, w_gu, w_gu, w_down]
    if final_norm:
        in_specs.append(_full_spec((1, D_MODEL)))
        args.append(g_final)
    return pl.pallas_call(
        functools.partial(_ffn_body, final_norm=final_norm),
        out_shape=jax.ShapeDtypeStruct((grp.rows, D_MODEL), F32),
        grid=(grp.rows // grp.tm, nf),
        in_specs=in_specs,
        out_specs=pl.BlockSpec((grp.tm, D_MODEL), lambda i, j: (i, 0)),
        scratch_shapes=[pltpu.VMEM((grp.tm, D_MODEL), BF16), pltpu.VMEM((grp.tm, D_MODEL), F32)],
        compiler_params=_params("parallel", "arbitrary"),
        name="ffn_half",
    )(*args)


def _hgrn_in_body(x_ref, shift_ref, scale_ref, g_ref, w_ref, lb_ref, qs_ref, k_ref, v_ref, og_ref, lg_ref,
                  *, layer):
    d = D_MODEL
    h = _mod_norm(x_ref[...], g_ref[...], scale_ref[...], shift_ref[...]).astype(BF16)
    lbp = lb_ref[...]
    e = jnp.exp(lbp - jnp.max(lbp, axis=0, keepdims=True))
    lb = jnp.sum(e[: layer + 1], axis=0, keepdims=True) / jnp.sum(e, axis=0, keepdims=True)

    q = _dot(h, w_ref[:, 0:d])
    qs_ref[...] = (q * _sigmoid(q)).astype(qs_ref.dtype)
    f = _dot(h, w_ref[:, d:2 * d])
    log_sig = jnp.minimum(f, 0.0) - jnp.log(1.0 + jnp.exp(-jnp.abs(f)))
    a = jnp.log(lb)
    b = jnp.log(1.0 - lb) + log_sig
    lg_ref[...] = jnp.maximum(a, b) + jnp.log(1.0 + jnp.exp(-jnp.abs(a - b)))
    k_ref[...] = ((1.0 - lb) / (1.0 + jnp.exp(f))).astype(k_ref.dtype)
    v_ref[...] = _dot(h, w_ref[:, 2 * d:3 * d]).astype(v_ref.dtype)
    g = _dot(h, w_ref[:, 3 * d:4 * d])
    og_ref[...] = (g * _sigmoid(g)).astype(og_ref.dtype)


def _hgrn_in(grp, x, mods, g_norm_row, w_in, hg_lb, layer, stream_dtype):
    sub = 1
    outs = ([jax.ShapeDtypeStruct((grp.rows, D_MODEL), stream_dtype)] * 4
            + [jax.ShapeDtypeStruct((grp.rows, D_MODEL), F32)])
    return pl.pallas_call(
        functools.partial(_hgrn_in_body, layer=layer),
        out_shape=outs,
        grid=(grp.rows // grp.tm,),
        in_specs=[
            _row_spec(grp, D_MODEL),
            _mod_spec(grp, mods, sub * 3 + 0),
            _mod_spec(grp, mods, sub * 3 + 1),
            _full_spec((1, D_MODEL)),
            pl.BlockSpec((None, D_MODEL, 4 * D_MODEL), lambda i: (layer, 0, 0)),
            _full_spec(hg_lb.shape),
        ],
        out_specs=[_row_spec(grp, D_MODEL)] * 5,
        compiler_params=_params("parallel"),
        name="hgrn_in",
    )(x, mods, mods, g_norm_row, w_in, hg_lb)


def _segmented_cumsum(x, seg):
    row = lax.broadcasted_iota(jnp.int32, x.shape, 0) & (seg - 1)
    s = 1
    while s < seg:
        x = x + jnp.where(row >= s, pltpu.roll(x, s, 0), 0.0)
        s *= 2
    return x


def _gla_body(q_ref, k_ref, v_ref, og_ref, lg_ref, go_ref, y_ref, s_out_ref, st_scr, b_scr):
    c, sub = GLA_CHUNK, GLA_SUB
    t = pl.program_id(1)

    @pl.when(t == 0)
    def _():
        st_scr[...] = jnp.zeros_like(st_scr)

    b_scr[...] = _segmented_cumsum(lg_ref[...], c)
    tri = lax.broadcasted_iota(jnp.int32, (c, c), 1) <= lax.broadcasted_iota(jnp.int32, (c, c), 0)

    def finish(r0, h, o):
        cols = slice(h * HG_DV, (h + 1) * HG_DV)
        y = _rms(o) * go_ref[:, cols] * og_ref[pl.ds(r0, c), cols].astype(F32)
        y_ref[pl.ds(r0, c), cols] = y.astype(y_ref.dtype)

    def chunk(ci, carry):
        r0 = pl.multiple_of(ci * c, c)
        b_all = b_scr[pl.ds(r0, c), :]
        ref_rows = [jnp.zeros((1, D_MODEL), F32)] + [b_all[i * sub - 1:i * sub, :] for i in range(1, c // sub)]
        worst = functools.reduce(jnp.maximum, [ref_rows[i] - b_all[(i + 1) * sub - 1:(i + 1) * sub, :]
                                               for i in range(c // sub)])
        safe = jnp.max(worst) <= GLA_SAFE_DECAY

        @pl.when(safe)
        def _():
            for h in range(HG_HEADS):
                cols = slice(h * HG_DK, (h + 1) * HG_DK)
                b = b_all[:, cols]
                b_last = b[c - 1:c, :]
                refs = [r[:, cols] for r in ref_rows]
                q = q_ref[pl.ds(r0, c), cols].astype(F32)
                k = k_ref[pl.ds(r0, c), cols].astype(F32)
                v = v_ref[pl.ds(r0, c), cols]
                st = st_scr[h]
                o = _dot_nt((q * jnp.exp(b)).astype(BF16), st.astype(BF16))
                rows = []
                for i in range(c // sub):
                    qi = (q[i * sub:(i + 1) * sub] * jnp.exp(b[i * sub:(i + 1) * sub] - refs[i])).astype(BF16)
                    ki = (k * jnp.exp(jnp.minimum(refs[i] - b, GLA_SAFE_DECAY))).astype(BF16)
                    rows.append(_dot_nt(qi, ki))
                att = jnp.where(tri, jnp.concatenate(rows, axis=0), 0.0)
                o = o + _dot(att.astype(BF16), v)
                kd = (k * jnp.exp(b_last - b)).astype(BF16)
                vt = v.astype(F32).T.astype(BF16)
                st_scr[h] = st * jnp.exp(b_last) + _dot(vt, kd)
                finish(r0, h, o)

        @pl.when(jnp.logical_not(safe))
        def _():
            lane = lax.broadcasted_iota(jnp.int32, (HG_DV, c), 1)
            sublane = lax.broadcasted_iota(jnp.int32, (c, HG_DK), 0)
            for h in range(HG_HEADS):
                cols = slice(h * HG_DK, (h + 1) * HG_DK)
                vt = v_ref[pl.ds(r0, c), cols].astype(F32).T
                qf = q_ref[pl.ds(r0, c), cols].astype(F32)
                kf = k_ref[pl.ds(r0, c), cols].astype(F32)
                df = jnp.exp(lg_ref[pl.ds(r0, c), cols])

                def token(ti, carry2, vt=vt, qf=qf, kf=kf, df=df):
                    st, ot = carry2
                    sel = lane == ti
                    pick = lambda a: jnp.sum(jnp.where(sublane == ti, a, 0.0), axis=0, keepdims=True)
                    v_col = jnp.sum(jnp.where(sel, vt, 0.0), axis=1, keepdims=True)
                    st = st * pick(df) + v_col * pick(kf)
                    o_col = jnp.sum(st * pick(qf), axis=1, keepdims=True)
                    return st, jnp.where(sel, o_col, ot)

                st, ot = lax.fori_loop(0, c, token, (st_scr[h], jnp.zeros((HG_DV, c), F32)))
                st_scr[h] = st
                finish(r0, h, ot.T)

        return carry

    lax.fori_loop(0, q_ref.shape[0] // c, chunk, 0)

    @pl.when(t == pl.num_programs(1) - 1)
    def _():
        for h in range(HG_HEADS):
            s_out_ref[h] = st_scr[h].T


def _gla_prompt(qs, k, v, og, lg, g_out_row, batch, seq):
    tb = GLA_BLOCK
    stream = pl.BlockSpec((None, tb, D_MODEL), lambda b, t: (b, t, 0))
    shp = (batch, seq, D_MODEL)
    return pl.pallas_call(
        _gla_body,
        out_shape=[jax.ShapeDtypeStruct(shp, BF16),
                   jax.ShapeDtypeStruct((batch, HG_HEADS, HG_DK, HG_DV), F32)],
        grid=(batch, seq // tb),
        in_specs=[stream, stream, stream, stream, stream, _full_spec((1, D_MODEL))],
        out_specs=[stream, pl.BlockSpec((None, HG_HEADS, HG_DK, HG_DV), lambda b, t: (b, 0, 0, 0))],
        scratch_shapes=[pltpu.VMEM((HG_HEADS, HG_DV, HG_DK), F32), pltpu.VMEM((tb, D_MODEL), F32)],
        compiler_params=_params("parallel", "arbitrary"),
        name="gla_chunked",
    )(qs.reshape(shp), k.reshape(shp), v.reshape(shp), og.reshape(shp), lg.reshape(shp), g_out_row)


def _gla_step_body(q_ref, k_ref, v_ref, og_ref, lg_ref, go_ref, s_ref, y_ref, s_out_ref):
    go = go_ref[...]

    def one(bi, carry):
        qt = q_ref[bi].astype(F32).T
        kt = k_ref[bi].astype(F32).T
        dt = jnp.exp(lg_ref[bi]).T
        v = v_ref[bi].astype(F32)
        outs = []
        for h in range(HG_HEADS):
            s = s_ref[bi, h] * dt[:, h:h + 1] + kt[:, h:h + 1] * v[h:h + 1, :]
            s_out_ref[bi, h] = s
            outs.append(jnp.sum(s * qt[:, h:h + 1], axis=0, keepdims=True))
        o = jnp.concatenate(outs, axis=0)
        y_ref[bi] = (_rms(o) * go * og_ref[bi].astype(F32)).astype(y_ref.dtype)
        return carry

    lax.fori_loop(0, q_ref.shape[0], one, 0)


def _gla_step(qs, k, v, og, lg, g_out_heads, state, layer, bb=8):
    rows = qs.shape[0]
    shp = (rows, HG_HEADS, HG_DK)
    stream = pl.BlockSpec((bb, HG_HEADS, HG_DK), lambda i: (i, 0, 0))
    return pl.pallas_call(
        _gla_step_body,
        out_shape=[jax.ShapeDtypeStruct(shp, F32),
                   jax.ShapeDtypeStruct((rows, HG_HEADS, HG_DK, HG_DV), F32)],
        grid=(rows // bb,),
        in_specs=[stream, stream, stream, stream, stream, _full_spec((HG_HEADS, HG_DV)),
                  pl.BlockSpec((None, bb, HG_HEADS, HG_DK, HG_DV), lambda i: (layer, i, 0, 0, 0))],
        out_specs=[stream, pl.BlockSpec((bb, HG_HEADS, HG_DK, HG_DV), lambda i: (i, 0, 0, 0))],
        compiler_params=_params("parallel"),
        name="gla_step",
    )(qs.reshape(shp), k.reshape(shp), v.reshape(shp), og.reshape(shp), lg.reshape(shp), g_out_heads, state)


def _proj_res_body(a_ref, w_ref, x_ref, gate_ref, o_ref):
    o_ref[...] = x_ref[...] + gate_ref[...] * _dot(a_ref[...].astype(BF16), w_ref[...])


def _proj_res(grp, a, w, layer, x, mods, sub):
    kdim = a.shape[1]
    return pl.pallas_call(
        _proj_res_body,
        out_shape=jax.ShapeDtypeStruct((grp.rows, D_MODEL), F32),
        grid=(grp.rows // grp.tm,),
        in_specs=[_row_spec(grp, kdim),
                  pl.BlockSpec((None, kdim, D_MODEL), lambda i: (layer, 0, 0)),
                  _row_spec(grp, D_MODEL),
                  _mod_spec(grp, mods, sub * 3 + 2)],
        out_specs=_row_spec(grp, D_MODEL),
        compiler_params=_params("parallel"),
        name="proj_residual",
    )(a, w, x, mods)


def _kv_body(x_ref, shift_ref, scale_ref, g_ref, wc_ref, wr_ref, wrr_ref, gl_ref, cos_ref, sin_ref,
             ckv_ref, kr_ref, kcat_ref, ckvt_ref):
    h = _mod_norm(x_ref[...], g_ref[...], scale_ref[...], shift_ref[...]).astype(BF16)
    ckv = _rms(_dot(h, wc_ref[...])) * gl_ref[...]
    kr = _dot(h, wr_ref[...]) * cos_ref[...] + _dot(h, wrr_ref[...]) * sin_ref[...]
    ckv_ref[...] = ckv
    kr_ref[...] = kr
    kcat_ref[:, 0:KV_RANK] = ckv.astype(BF16)
    kcat_ref[:, KV_RANK:QK_LAT] = kr.astype(BF16)
    ckvt_ref[...] = ckv.T.astype(BF16)


def _shared_kv(grp, x, kvmods, g_norm_row, w_c, w_r, w_rr, g_latent_row, cos_t, sin_t):
    tps = grp.tiles_per_seq
    return pl.pallas_call(
        _kv_body,
        out_shape=[jax.ShapeDtypeStruct((grp.rows, KV_RANK), F32),
                   jax.ShapeDtypeStruct((grp.rows, QK_ROPE), F32),
                   jax.ShapeDtypeStruct((grp.rows, QK_LAT), BF16),
                   jax.ShapeDtypeStruct((grp.rows // grp.seq, KV_RANK, grp.seq), BF16)],
        grid=(grp.rows // grp.tm,),
        in_specs=[_row_spec(grp, D_MODEL),
                  _mod_spec(grp, kvmods, 0),
                  _mod_spec(grp, kvmods, 1),
                  _full_spec((1, D_MODEL)),
                  _full_spec(w_c.shape), _full_spec(w_r.shape), _full_spec(w_rr.shape),
                  _full_spec((1, KV_RANK)),
                  _pos_spec(grp, QK_ROPE), _pos_spec(grp, QK_ROPE)],
        out_specs=[_row_spec(grp, KV_RANK), _row_spec(grp, QK_ROPE), _row_spec(grp, QK_LAT),
                   pl.BlockSpec((None, KV_RANK, grp.tm), lambda i: (i // tps, 0, i % tps))],
        compiler_params=_params("parallel"),
        name="shared_kv",
    )(x, kvmods, kvmods, g_norm_row, w_c, w_r, w_rr, g_latent_row, cos_t, sin_t)


def _mla_q_body(x_ref, shift_ref, scale_ref, g_ref, wdq_ref, gq_ref, wn_ref, wrt_ref, wrrt_ref, wukt_ref,
                cos_ref, sin_ref, q_ref):
    tm = x_ref.shape[0]
    h = _mod_norm(x_ref[...], g_ref[...], scale_ref[...], shift_ref[...]).astype(BF16)
    qc = (_rms(_dot(h, wdq_ref[...])) * gq_ref[...]).astype(BF16)
    qn = _dot(qc, wn_ref[...])
    qrt = (_dot_nt(wrt_ref[...], qc) * cos_ref[...] + _dot_nt(wrrt_ref[...], qc) * sin_ref[...]) * SM_SCALE
    for hd in range(MLA_HEADS):
        cols = slice(hd * tm, (hd + 1) * tm)
        qlt = _dot_nt(wukt_ref[hd], qn[:, hd * QK_NOPE:(hd + 1) * QK_NOPE].astype(BF16)) * SM_SCALE
        q_ref[0:KV_RANK, cols] = qlt.astype(q_ref.dtype)
        q_ref[KV_RANK:QK_LAT, cols] = qrt[hd * QK_ROPE:(hd + 1) * QK_ROPE, :].astype(q_ref.dtype)


def _mla_q(grp, x, mods, g_norm_row, w_dq, g_q_row, w_n, w_rt, w_rrt, w_ukt, cos_t, sin_t, layer_b):
    sub = 1
    nseq = grp.rows // grp.seq
    tps = grp.tiles_per_seq
    width = MLA_HEADS * QK_ROPE
    sel = lambda a: pl.BlockSpec((None,) + a.shape[1:], lambda i: (layer_b,) + (0,) * (a.ndim - 1))
    if grp.per_row:
        pos = pl.BlockSpec((width, grp.tm), lambda i: (0, i))
    else:
        pos = pl.BlockSpec((width, grp.tm), lambda i: (0, i % tps))
    return pl.pallas_call(
        _mla_q_body,
        out_shape=jax.ShapeDtypeStruct((nseq, tps, QK_LAT, MLA_HEADS * grp.tm), BF16),
        grid=(grp.rows // grp.tm,),
        in_specs=[_row_spec(grp, D_MODEL),
                  _mod_spec(grp, mods, sub * 3 + 0),
                  _mod_spec(grp, mods, sub * 3 + 1),
                  _full_spec((1, D_MODEL)),
                  sel(w_dq), sel(g_q_row), sel(w_n), sel(w_rt), sel(w_rrt),
                  _full_spec(w_ukt.shape),
                  pos, pos],
        out_specs=pl.BlockSpec((None, None, QK_LAT, MLA_HEADS * grp.tm), lambda i: (i // tps, i % tps, 0, 0)),
        compiler_params=_params("parallel"),
        name="mla_q",
    )(x, mods, mods, g_norm_row, w_dq, g_q_row, w_n, w_rt, w_rrt, w_ukt, cos_t, sin_t)


def _attn_body(qi_ref, kj_ref, qt_ref, k_ref, vt_ref, o_ref, m_scr, l_scr, acc_scr, *, tq):
    p = pl.program_id(1)
    i = qi_ref[p]
    j = kj_ref[p]

    @pl.when(j == 0)
    def _():
        m_scr[...] = jnp.full_like(m_scr, -jnp.inf)
        l_scr[...] = jnp.zeros_like(l_scr)
        acc_scr[...] = jnp.zeros_like(acc_scr)

    def update(masked):
        s = _dot(k_ref[...], qt_ref[...])
        if masked:
            kpos = lax.broadcasted_iota(jnp.int32, s.shape, 0)
            qpos = lax.broadcasted_iota(jnp.int32, s.shape, 1) & (tq - 1)
            s = jnp.where(kpos <= qpos, s, -jnp.inf)
        m_prev = m_scr[...]
        m_new = jnp.maximum(m_prev, jnp.max(s, axis=0, keepdims=True))
        alpha = jnp.exp(m_prev - m_new)
        pr = jnp.exp(s - m_new)
        l_scr[...] = alpha * l_scr[...] + jnp.sum(pr, axis=0, keepdims=True)
        acc_scr[...] = alpha * acc_scr[...] + _dot(vt_ref[...], pr.astype(BF16))
        m_scr[...] = m_new

    @pl.when(j < i)
    def _():
        update(False)

    @pl.when(j == i)
    def _():
        update(True)
        out = acc_scr[...] / l_scr[...]
        for hd in range(MLA_HEADS):
            o_ref[:, hd * KV_RANK:(hd + 1) * KV_RANK] = out[:, hd * tq:(hd + 1) * tq].T.astype(o_ref.dtype)


def _attend_prompt(qt, kcat, ckvt, batch, seq, tq):
    nq = seq // tq
    pairs = [(i, j) for i in range(nq) for j in range(i + 1)]
    qi = jnp.asarray(np.array([p[0] for p in pairs], np.int32))
    kj = jnp.asarray(np.array([p[1] for p in pairs], np.int32))
    cols = MLA_HEADS * tq
    return pl.pallas_call(
        functools.partial(_attn_body, tq=tq),
        out_shape=jax.ShapeDtypeStruct((batch, seq, MLA_HEADS * KV_RANK), BF16),
        grid_spec=pltpu.PrefetchScalarGridSpec(
            num_scalar_prefetch=2,
            grid=(batch, len(pairs)),
            in_specs=[pl.BlockSpec((None, None, QK_LAT, cols), lambda b, p, qi, kj: (b, qi[p], 0, 0)),
                      pl.BlockSpec((None, tq, QK_LAT), lambda b, p, qi, kj: (b, kj[p], 0)),
                      pl.BlockSpec((None, KV_RANK, tq), lambda b, p, qi, kj: (b, 0, kj[p]))],
            out_specs=pl.BlockSpec((None, tq, MLA_HEADS * KV_RANK), lambda b, p, qi, kj: (b, qi[p], 0)),
            scratch_shapes=[pltpu.VMEM((1, cols), F32), pltpu.VMEM((1, cols), F32),
                            pltpu.VMEM((KV_RANK, cols), F32)],
        ),
        compiler_params=_params("parallel", "arbitrary"),
        name="attend_prompt",
    )(qi, kj, qt, kcat.reshape(batch, seq, QK_LAT), ckvt)


PAGED_CHUNK_PAGES = 32
PAGED_SLOTS = 4


def _paged_body(pt_ref, q_ref, cn_ref, kn_ref, ckv_hbm, krt_hbm, o_ref, cbuf, kbuf, sems, *, n_pages):
    ch = PAGED_CHUNK_PAGES
    nc = n_pages // ch
    ahead = PAGED_SLOTS - 1
    nseq = pl.num_programs(0)
    b = pl.program_id(0)

    def copies(g, slot):
        out = []
        for pg in range(ch):
            page = pt_ref[g * ch + pg]
            dst = pl.ds(pg * PAGE_SIZE, PAGE_SIZE)
            out.append(pltpu.make_async_copy(ckv_hbm.at[page], cbuf.at[slot, dst, :], sems.at[0, slot]))
            out.append(pltpu.make_async_copy(krt_hbm.at[page], kbuf.at[slot, pg], sems.at[1, slot]))
        return out

    def start(g, slot):
        for cp in copies(g, slot):
            cp.start()

    def wait(slot):
        for cp in copies(0, slot):
            cp.wait()

    @pl.when(b == 0)
    def _():
        for c in range(ahead):
            start(c, c)

    q = q_ref[...]
    ql = q[:, 0:KV_RANK]
    qr = q[:, KV_RANK:QK_LAT]
    cn = cn_ref[...]
    kn = kn_ref[...]
    m = jnp.sum(ql * cn, axis=1, keepdims=True) + jnp.sum(qr * kn, axis=1, keepdims=True)
    l = jnp.ones_like(m)
    acc = jnp.broadcast_to(cn, (MLA_HEADS, KV_RANK))
    qlb = ql.astype(BF16)
    qrb = qr.astype(BF16)
    for c in range(nc):
        slot = c % PAGED_SLOTS
        g = b * nc + c
        nxt = (c + ahead) % PAGED_SLOTS
        if c + ahead < nc:
            start(g + ahead, nxt)
        else:
            @pl.when(b + 1 < nseq)
            def _():
                start(g + ahead, nxt)
        wait(slot)
        kc = cbuf[slot].astype(BF16)
        s_rope = [_dot(qrb, kbuf[slot, pg].astype(BF16)) for pg in range(ch)]
        s = _dot_nt(qlb, kc) + jnp.concatenate(s_rope, axis=1)
        m_new = jnp.maximum(m, jnp.max(s, axis=1, keepdims=True))
        alpha = jnp.exp(m - m_new)
        pr = jnp.exp(s - m_new)
        l = alpha * l + jnp.sum(pr, axis=1, keepdims=True)
        acc = alpha * acc + _dot(pr.astype(BF16), kc)
        m = m_new
    o_ref[...] = acc / l


def _attend_paged(q, ckv_new, krope_new, cache_ckv, cache_krope_t, page_table):
    nseq, n_pages = page_table.shape
    ch = PAGED_CHUNK_PAGES
    assert n_pages % (PAGED_SLOTS * ch) == 0
    per_seq = lambda *tail: pl.BlockSpec((None,) + tail, lambda i, pt: (i,) + (0,) * len(tail))
    return pl.pallas_call(
        functools.partial(_paged_body, n_pages=n_pages),
        out_shape=jax.ShapeDtypeStruct((nseq, MLA_HEADS, KV_RANK), F32),
        grid_spec=pltpu.PrefetchScalarGridSpec(
            num_scalar_prefetch=1,
            grid=(nseq,),
            in_specs=[per_seq(MLA_HEADS, QK_LAT), per_seq(1, KV_RANK), per_seq(1, QK_ROPE),
                      pl.BlockSpec(memory_space=pl.ANY), pl.BlockSpec(memory_space=pl.ANY)],
            out_specs=per_seq(MLA_HEADS, KV_RANK),
            scratch_shapes=[pltpu.VMEM((PAGED_SLOTS, ch * PAGE_SIZE, KV_RANK), F32),
                            pltpu.VMEM((PAGED_SLOTS, ch, QK_ROPE, PAGE_SIZE), F32),
                            pltpu.SemaphoreType.DMA((2, PAGED_SLOTS))],
        ),
        compiler_params=_params("arbitrary"),
        name="attend_paged",
    )(page_table.reshape(-1), q, ckv_new.reshape(nseq, 1, KV_RANK), krope_new.reshape(nseq, 1, QK_ROPE),
      cache_ckv, cache_krope_t)


def _attn_out_body(o_ref, wuv_ref, wo_ref, x_ref, gate_ref, out_ref):
    parts = [_dot(o_ref[:, hd * KV_RANK:(hd + 1) * KV_RANK], wuv_ref[hd]).astype(BF16)
             for hd in range(MLA_HEADS)]
    o = jnp.concatenate(parts, axis=1)
    out_ref[...] = x_ref[...] + gate_ref[...] * _dot(o, wo_ref[...])


def _attn_out(grp, o_lat, w_uv_t, w_out, layer_b, x, mods, sub):
    width = MLA_HEADS * KV_RANK
    return pl.pallas_call(
        _attn_out_body,
        out_shape=jax.ShapeDtypeStruct((grp.rows, D_MODEL), F32),
        grid=(grp.rows // grp.tm,),
        in_specs=[_row_spec(grp, width),
                  _full_spec(w_uv_t.shape),
                  pl.BlockSpec((None, MLA_HEADS * V_HEAD, D_MODEL), lambda i: (layer_b, 0, 0)),
                  _row_spec(grp, D_MODEL),
                  _mod_spec(grp, mods, sub * 3 + 2)],
        out_specs=_row_spec(grp, D_MODEL),
        compiler_params=_params("parallel"),
        name="attn_out",
    )(o_lat, w_uv_t, w_out, x, mods)


def _rope_tables(pos):
    half = QK_ROPE // 2
    inv_freq = ROPE_THETA ** (-2.0 * jnp.arange(half, dtype=F32) / QK_ROPE)
    ang = pos.astype(F32)[:, None] * inv_freq[None, :]
    cos, sin = jnp.cos(ang), jnp.sin(ang)
    return jnp.concatenate([cos, cos], axis=-1), jnp.concatenate([-sin, sin], axis=-1)


def _swap_halves(w):
    half = QK_ROPE // 2
    return jnp.concatenate([w[..., half:], w[..., :half]], axis=-1)


def kernel(x_prompt, x_sample, c_prompt, c_sample, state_hgrn, cache_ckv, cache_krope, page_table, w_ada, b_ada, g_norm, w_ffn_gu, w_ffn_down, hg_w_in, hg_lb, hg_g_out, hg_w_out, kv_w_ada, kv_b_ada, kv_g_norm, kv_w_down, kv_g_latent, kv_w_uk, kv_w_uv, q_w_down, q_g_norm, q_w_up, attn_w_out, g_final):
    batch, seq, d = x_prompt.shape
    nseq = x_sample.shape[0]
    n_b = DEPTH - N_A_LAYERS

    w_gu = w_ffn_gu.astype(BF16)
    w_dn = w_ffn_down.astype(BF16)
    w_in = hg_w_in.astype(BF16)
    w_ho = hg_w_out.astype(BF16)
    w_ao = attn_w_out.astype(BF16)
    w_kc = kv_w_down[:, :KV_RANK].astype(BF16)
    w_kr = kv_w_down[:, KV_RANK:].astype(BF16)
    w_krr = _swap_halves(kv_w_down[:, KV_RANK:]).astype(BF16)
    w_dq = q_w_down.astype(BF16)
    w_qn = q_w_up[..., :QK_NOPE].reshape(n_b, Q_RANK, MLA_HEADS * QK_NOPE).astype(BF16)
    w_qr = q_w_up[..., QK_NOPE:].reshape(n_b, Q_RANK, MLA_HEADS * QK_ROPE)
    w_qrr = _swap_halves(q_w_up[..., QK_NOPE:]).reshape(n_b, Q_RANK, MLA_HEADS * QK_ROPE)
    w_qrt = w_qr.transpose(0, 2, 1).astype(BF16)
    w_qrrt = w_qrr.transpose(0, 2, 1).astype(BF16)
    w_ukt = kv_w_uk.transpose(1, 0, 2).astype(BF16)
    w_uv_t = kv_w_uv.transpose(1, 0, 2).astype(BF16)
    cache_krope_t = cache_krope.transpose(0, 2, 1)
    g_q_rows = q_g_norm.reshape(n_b, 1, Q_RANK)
    g_out_rows = hg_g_out.reshape(N_A_LAYERS, 1, D_MODEL)

    pad = 8
    c_all = jnp.concatenate([c_prompt, jnp.zeros((pad - batch, d), F32), c_sample], axis=0)
    mods_all = [_ada(c_all, w_ada, b_ada.reshape(DEPTH, 1, -1), l) for l in range(DEPTH)]
    kvm_all = _ada(c_all, kv_w_ada[None], kv_b_ada.reshape(1, 1, -1), 0)

    def trunk(grp, x, mods_l, kvmods, pos, mix, attend, stream_dtype):
        grp_half = grp._replace(tm=max(grp.tm // 2, 128))
        grp_q = grp._replace(tm=min(grp.tm, ATTN_TQ))
        cos_k, sin_k = _rope_tables(pos)
        cos_qt, sin_qt = jnp.tile(cos_k, (1, MLA_HEADS)).T, jnp.tile(sin_k, (1, MLA_HEADS)).T
        states = []
        ckv = krope = kcat = ckvt = None
        for l in range(DEPTH):
            mods = mods_l[l]
            gn = lambda j: g_norm[l, j].reshape(1, d)
            if l == N_A_LAYERS:
                ckv, krope, kcat, ckvt = _shared_kv(grp, x, kvmods, kv_g_norm.reshape(1, d), w_kc, w_kr, w_krr,
                                                    kv_g_latent.reshape(1, KV_RANK), cos_k, sin_k)
            x = _ffn(grp, x, mods, 0, gn(0), w_gu, w_dn, l, 0)
            if l < N_A_LAYERS:
                qs, k, v, og, lg = _hgrn_in(grp_half, x, mods, gn(1), w_in, hg_lb, l, stream_dtype)
                y, s = mix(l, qs, k, v, og, lg)
                states.append(s)
                x = _proj_res(grp, y, w_ho, l, x, mods, 1)
            else:
                lb_ = l - N_A_LAYERS
                qt = _mla_q(grp_q, x, mods, gn(1), w_dq, g_q_rows, w_qn, w_qrt, w_qrrt, w_ukt, cos_qt, sin_qt,
                            lb_)
                o_lat = attend(qt, ckv, krope, kcat, ckvt)
                x = _attn_out(grp, o_lat, w_uv_t, w_ao, lb_, x, mods, 1)
            x = _ffn(grp, x, mods, 2, gn(2), w_gu, w_dn, l, 1,
                     g_final=g_final.reshape(1, d) if l == DEPTH - 1 else None)
        return x, jnp.stack(states), ckv, krope

    grp_p = Group(rows=batch * seq, seq=seq, tm=1024, per_row=False)
    mods_p = [m[:pad].reshape(pad, 1, -1) for m in mods_all]
    kvm_p = kvm_all[:pad].reshape(pad, 1, -1)

    def mix_p(l, qs, k, v, og, lg):
        y, s = _gla_prompt(qs, k, v, og, lg, g_out_rows[l], batch, seq)
        return y.reshape(batch * seq, d), s

    def attend_p(qt, ckv, krope, kcat, ckvt):
        o = _attend_prompt(qt, kcat, ckvt, batch, seq, ATTN_TQ)
        return o.reshape(batch * seq, MLA_HEADS * KV_RANK)

    y_p, st_p, ckv_p, kr_p = trunk(grp_p, x_prompt.reshape(batch * seq, d), mods_p, kvm_p,
                                   jnp.arange(seq), mix_p, attend_p, BF16)

    grp_s = Group(rows=nseq, seq=nseq, tm=nseq, per_row=True)
    mods_s = [m[pad:] for m in mods_all]
    kvm_s = kvm_all[pad:]

    def mix_s(l, qs, k, v, og, lg):
        y, s = _gla_step(qs, k, v, og, lg, hg_g_out[l].reshape(HG_HEADS, HG_DV), state_hgrn, l)
        return y.reshape(nseq, d), s

    def attend_s(qt, ckv, krope, kcat, ckvt):
        qf = qt.reshape(QK_LAT, MLA_HEADS, nseq).transpose(2, 1, 0).astype(F32)
        o = _attend_paged(qf, ckv, krope, cache_ckv, cache_krope_t, page_table)
        return o.reshape(nseq, MLA_HEADS * KV_RANK).astype(BF16)

    y_s, st_s, ckv_s, kr_s = trunk(grp_s, x_sample.reshape(nseq, d), mods_s, kvm_s,
                                   jnp.full((nseq,), PAST_LEN), mix_s, attend_s, F32)

    return (y_p.reshape(batch, seq, d), y_s.reshape(nseq, 1, d),
            st_p.astype(state_hgrn.dtype), st_s.astype(state_hgrn.dtype),
            ckv_p.reshape(batch, seq, KV_RANK), kr_p.reshape(batch, seq, QK_ROPE),
            ckv_s.reshape(nseq, 1, KV_RANK), kr_s.reshape(nseq, 1, QK_ROPE))
```

```python
import functools
from typing import NamedTuple

import jax
import jax.numpy as jnp
import numpy as np
from jax import lax
from jax.experimental import pallas as pl
from jax.experimental.pallas import tpu as pltpu

D_MODEL = 1024
DEPTH = 2
N_A_LAYERS = DEPTH // 2
PAST_LEN = 16384
PAGE_SIZE = 128
D_FF = 2816
HG_HEADS = 8
HG_DK = D_MODEL // HG_HEADS
HG_DV = D_MODEL // HG_HEADS
MLA_HEADS = 8
QK_NOPE = 128
QK_ROPE = 64
V_HEAD = 128
KV_RANK = 256
Q_RANK = 384
QK_LAT = KV_RANK + QK_ROPE
ROPE_THETA = 10000.0
EPS = 1e-6
SM_SCALE = (QK_NOPE + QK_ROPE) ** -0.5

BF16 = jnp.bfloat16
F32 = jnp.float32

V7X_VMEM_BYTES = 64 * 1024 * 1024
VMEM_LIMIT = V7X_VMEM_BYTES * 3 // 4

GLA_CHUNK = 64
GLA_SUB = 16
GLA_BLOCK = 256
GLA_SAFE_DECAY = 60.0

ATTN_TQ = 512
VT_ROWS = KV_RANK + 16

NT_DIMS = (((1,), (1,)), ((), ()))


class Group(NamedTuple):
    rows: int
    seq: int
    tm: int
    per_row: bool

    @property
    def tiles_per_seq(self):
        return self.seq // self.tm


def _params(*sem):
    return pltpu.CompilerParams(dimension_semantics=sem, vmem_limit_bytes=VMEM_LIMIT)


def _dot(a, b):
    return jnp.dot(a, b, preferred_element_type=F32)


def _dot_nt(a, b):
    return lax.dot_general(a, b, NT_DIMS, preferred_element_type=F32)


def _sigmoid(x):
    return 1.0 / (1.0 + jnp.exp(-x))


def _rms(x):
    return x * lax.rsqrt(jnp.mean(x * x, axis=-1, keepdims=True) + EPS)


def _mod_norm(x, g, scale, shift):
    return (_rms(x) * g) * (1.0 + scale) + shift


def _mod_spec(grp, mods, col):
    if grp.per_row:
        return pl.BlockSpec((grp.tm, D_MODEL), lambda *g: (g[0], col))
    tps = grp.tiles_per_seq
    return pl.BlockSpec((None, 1, D_MODEL), lambda *g: (g[0] // tps, 0, col))


def _pos_spec(grp, width):
    if grp.per_row:
        return pl.BlockSpec((grp.tm, width), lambda *g: (g[0], 0))
    tps = grp.tiles_per_seq
    return pl.BlockSpec((grp.tm, width), lambda *g: (g[0] % tps, 0))


def _row_spec(grp, width):
    return pl.BlockSpec((grp.tm, width), lambda *g: (g[0], 0))


def _full_spec(shape):
    nd = len(shape)
    return pl.BlockSpec(shape, lambda *g: (0,) * nd)


def _ada_body(c_ref, w_ref, b_ref, o_ref):
    c = c_ref[...]
    sc = (c * _sigmoid(c)).astype(BF16)
    o_ref[...] = _dot(sc, w_ref[...].astype(BF16)) + b_ref[...]


def _ada(c_all, w, b, layer, tn=1024):
    rows = c_all.shape[0]
    n = w.shape[-1]
    return pl.pallas_call(
        _ada_body,
        out_shape=jax.ShapeDtypeStruct((rows, n), F32),
        grid=(n // tn,),
        in_specs=[
            _full_spec((rows, D_MODEL)),
            pl.BlockSpec((None, D_MODEL, tn), lambda j: (layer, 0, j)),
            pl.BlockSpec((None, 1, tn), lambda j: (layer, 0, j)),
        ],
        out_specs=pl.BlockSpec((rows, tn), lambda j: (0, j)),
        compiler_params=_params("arbitrary"),
        name="ada_proj",
    )(c_all, w, b)


def _ffn_body(x_ref, shift_ref, scale_ref, gate_ref, g_ref, wg_ref, wu_ref, wd_ref, *rest, final_norm):
    if final_norm:
        gf_ref, o_ref, h_scr, acc_scr = rest
    else:
        o_ref, h_scr, acc_scr = rest
    j = pl.program_id(1)

    @pl.when(j == 0)
    def _():
        h = _mod_norm(x_ref[...], g_ref[...], scale_ref[...], shift_ref[...])
        h_scr[...] = h.astype(BF16)
        acc_scr[...] = jnp.zeros_like(acc_scr)

    h = h_scr[...]
    a = _dot(h, wg_ref[...].astype(BF16))
    b = _dot(h, wu_ref[...].astype(BF16))
    act = (a * _sigmoid(a)) * b
    acc_scr[...] += _dot(act.astype(BF16), wd_ref[...].astype(BF16))

    @pl.when(j == pl.num_programs(1) - 1)
    def _():
        y = x_ref[...] + (0.5 * gate_ref[...]) * acc_scr[...]
        if final_norm:
            y = _rms(y) * gf_ref[...]
        o_ref[...] = y


def _ffn(grp, x, mods, sub, g_norm_row, w_gu, w_down, layer, half, g_final=None, tf=256):
    nf = D_FF // tf
    final_norm = g_final is not None
    in_specs = [
        _row_spec(grp, D_MODEL),
        _mod_spec(grp, mods, sub * 3 + 0),
        _mod_spec(grp, mods, sub * 3 + 1),
        _mod_spec(grp, mods, sub * 3 + 2),
        _full_spec((1, D_MODEL)),
        pl.BlockSpec((None, None, D_MODEL, tf), lambda i, j: (layer, half, 0, j)),
        pl.BlockSpec((None, None, D_MODEL, tf), lambda i, j: (layer, half, 0, j + nf)),
        pl.BlockSpec((None, None, tf, D_MODEL), lambda i, j: (layer, half, j, 0)),
    ]
    args = [x, mods, mods, mods, g_norm_row, w_gu, w_gu, w_down]
    if final_norm:
        in_specs.append(_full_spec((1, D_MODEL)))
        args.append(g_final)
    return pl.pallas_call(
        functools.partial(_ffn_body, final_norm=final_norm),
        out_shape=jax.ShapeDtypeStruct((grp.rows, D_MODEL), F32),
        grid=(grp.rows // grp.tm, nf),
        in_specs=in_specs,
        out_specs=pl.BlockSpec((grp.tm, D_MODEL), lambda i, j: (i, 0)),
        scratch_shapes=[pltpu.VMEM((grp.tm, D_MODEL), BF16), pltpu.VMEM((grp.tm, D_MODEL), F32)],
        compiler_params=_params("parallel", "arbitrary"),
        name="ffn_half",
    )(*args)


def _hgrn_in_body(x_ref, shift_ref, scale_ref, g_ref, w_ref, lb_ref, qs_ref, k_ref, v_ref, og_ref, lg_ref,
                  *, layer):
    d = D_MODEL
    h = _mod_norm(x_ref[...], g_ref[...], scale_ref[...], shift_ref[...]).astype(BF16)
    lbp = lb_ref[...]
    e = jnp.exp(lbp - jnp.max(lbp, axis=0, keepdims=True))
    lb = jnp.sum(e[: layer + 1], axis=0, keepdims=True) / jnp.sum(e, axis=0, keepdims=True)

    q = _dot(h, w_ref[:, 0:d])
    qs_ref[...] = (q * _sigmoid(q)).astype(qs_ref.dtype)
    f = _dot(h, w_ref[:, d:2 * d])
    log_sig = jnp.minimum(f, 0.0) - jnp.log(1.0 + jnp.exp(-jnp.abs(f)))
    a = jnp.log(lb)
    b = jnp.log(1.0 - lb) + log_sig
    lg_ref[...] = jnp.maximum(a, b) + jnp.log(1.0 + jnp.exp(-jnp.abs(a - b)))
    k_ref[...] = ((1.0 - lb) / (1.0 + jnp.exp(f))).astype(k_ref.dtype)
    v_ref[...] = _dot(h, w_ref[:, 2 * d:3 * d]).astype(v_ref.dtype)
    g = _dot(h, w_ref[:, 3 * d:4 * d])
    og_ref[...] = (g * _sigmoid(g)).astype(og_ref.dtype)


def _hgrn_in(grp, x, mods, g_norm_row, w_in, hg_lb, layer, stream_dtype):
    sub = 1
    outs = ([jax.ShapeDtypeStruct((grp.rows, D_MODEL), stream_dtype)] * 4
            + [jax.ShapeDtypeStruct((grp.rows, D_MODEL), F32)])
    return pl.pallas_call(
        functools.partial(_hgrn_in_body, layer=layer),
        out_shape=outs,
        grid=(grp.rows // grp.tm,),
        in_specs=[
            _row_spec(grp, D_MODEL),
            _mod_spec(grp, mods, sub * 3 + 0),
            _mod_spec(grp, mods, sub * 3 + 1),
            _full_spec((1, D_MODEL)),
            pl.BlockSpec((None, D_MODEL, 4 * D_MODEL), lambda i: (layer, 0, 0)),
            _full_spec(hg_lb.shape),
        ],
        out_specs=[_row_spec(grp, D_MODEL)] * 5,
        compiler_params=_params("parallel"),
        name="hgrn_in",
    )(x, mods, mods, g_norm_row, w_in, hg_lb)


def _segmented_cumsum(x, seg):
    row = lax.broadcasted_iota(jnp.int32, x.shape, 0) & (seg - 1)
    s = 1
    while s < seg:
        x = x + jnp.where(row >= s, pltpu.roll(x, s, 0), 0.0)
        s *= 2
    return x


def _gla_body(q_ref, k_ref, v_ref, og_ref, lg_ref, go_ref, y_ref, s_out_ref, st_scr, b_scr):
    c, sub = GLA_CHUNK, GLA_SUB
    t = pl.program_id(1)

    @pl.when(t == 0)
    def _():
        st_scr[...] = jnp.zeros_like(st_scr)

    b_scr[...] = _segmented_cumsum(lg_ref[...], c)
    tri = lax.broadcasted_iota(jnp.int32, (c, c), 1) <= lax.broadcasted_iota(jnp.int32, (c, c), 0)

    def finish(r0, h, o):
        cols = slice(h * HG_DV, (h + 1) * HG_DV)
        y = _rms(o) * go_ref[:, cols] * og_ref[pl.ds(r0, c), cols].astype(F32)
        y_ref[pl.ds(r0, c), cols] = y.astype(y_ref.dtype)

    def chunk(ci, carry):
        r0 = pl.multiple_of(ci * c, c)
        b_all = b_scr[pl.ds(r0, c), :]
        ref_rows = [jnp.zeros((1, D_MODEL), F32)] + [b_all[i * sub - 1:i * sub, :] for i in range(1, c // sub)]
        worst = functools.reduce(jnp.maximum, [ref_rows[i] - b_all[(i + 1) * sub - 1:(i + 1) * sub, :]
                                               for i in range(c // sub)])
        safe = jnp.max(worst) <= GLA_SAFE_DECAY

        @pl.when(safe)
        def _():
            for h in range(HG_HEADS):
                cols = slice(h * HG_DK, (h + 1) * HG_DK)
                b = b_all[:, cols]
                b_last = b[c - 1:c, :]
                refs = [r[:, cols] for r in ref_rows]
                q = q_ref[pl.ds(r0, c), cols].astype(F32)
                k = k_ref[pl.ds(r0, c), cols].astype(F32)
                v = v_ref[pl.ds(r0, c), cols]
                st = st_scr[h]
                o = _dot_nt((q * jnp.exp(b)).astype(BF16), st.astype(BF16))
                rows = []
                for i in range(c // sub):
                    qi = (q[i * sub:(i + 1) * sub] * jnp.exp(b[i * sub:(i + 1) * sub] - refs[i])).astype(BF16)
                    ki = (k * jnp.exp(jnp.minimum(refs[i] - b, GLA_SAFE_DECAY))).astype(BF16)
                    rows.append(_dot_nt(qi, ki))
                att = jnp.where(tri, jnp.concatenate(rows, axis=0), 0.0)
                o = o + _dot(att.astype(BF16), v)
                kd = (k * jnp.exp(b_last - b)).astype(BF16)
                vt = v.astype(F32).T.astype(BF16)
                st_scr[h] = st * jnp.exp(b_last) + _dot(vt, kd)
                finish(r0, h, o)

        @pl.when(jnp.logical_not(safe))
        def _():
            lane = lax.broadcasted_iota(jnp.int32, (HG_DV, c), 1)
            sublane = lax.broadcasted_iota(jnp.int32, (c, HG_DK), 0)
            for h in range(HG_HEADS):
                cols = slice(h * HG_DK, (h + 1) * HG_DK)
                vt = v_ref[pl.ds(r0, c), cols].astype(F32).T
                qf = q_ref[pl.ds(r0, c), cols].astype(F32)
                kf = k_ref[pl.ds(r0, c), cols].astype(F32)
                df = jnp.exp(lg_ref[pl.ds(r0, c), cols])

                def token(ti, carry2, vt=vt, qf=qf, kf=kf, df=df):
                    st, ot = carry2
                    sel = lane == ti
                    pick = lambda a: jnp.sum(jnp.where(sublane == ti, a, 0.0), axis=0, keepdims=True)
                    v_col = jnp.sum(jnp.where(sel, vt, 0.0), axis=1, keepdims=True)
                    st = st * pick(df) + v_col * pick(kf)
                    o_col = jnp.sum(st * pick(qf), axis=1, keepdims=True)
                    return st, jnp.where(sel, o_col, ot)

                st, ot = lax.fori_loop(0, c, token, (st_scr[h], jnp.zeros((HG_DV, c), F32)))
                st_scr[h] = st
                finish(r0, h, ot.T)

        return carry

    lax.fori_loop(0, q_ref.shape[0] // c, chunk, 0)

    @pl.when(t == pl.num_programs(1) - 1)
    def _():
        for h in range(HG_HEADS):
            s_out_ref[h] = st_scr[h].T


def _gla_prompt(qs, k, v, og, lg, g_out_row, batch, seq):
    tb = GLA_BLOCK
    stream = pl.BlockSpec((None, tb, D_MODEL), lambda b, t: (b, t, 0))
    shp = (batch, seq, D_MODEL)
    return pl.pallas_call(
        _gla_body,
        out_shape=[jax.ShapeDtypeStruct(shp, BF16),
                   jax.ShapeDtypeStruct((batch, HG_HEADS, HG_DK, HG_DV), F32)],
        grid=(batch, seq // tb),
        in_specs=[stream, stream, stream, stream, stream, _full_spec((1, D_MODEL))],
        out_specs=[stream, pl.BlockSpec((None, HG_HEADS, HG_DK, HG_DV), lambda b, t: (b, 0, 0, 0))],
        scratch_shapes=[pltpu.VMEM((HG_HEADS, HG_DV, HG_DK), F32), pltpu.VMEM((tb, D_MODEL), F32)],
        compiler_params=_params("parallel", "arbitrary"),
        name="gla_chunked",
    )(qs.reshape(shp), k.reshape(shp), v.reshape(shp), og.reshape(shp), lg.reshape(shp), g_out_row)


def _gla_step_body(q_ref, k_ref, v_ref, og_ref, lg_ref, go_ref, s_ref, y_ref, s_out_ref):
    go = go_ref[...]

    def one(bi, carry):
        qt = q_ref[bi].astype(F32).T
        kt = k_ref[bi].astype(F32).T
        dt = jnp.exp(lg_ref[bi]).T
        v = v_ref[bi].astype(F32)
        outs = []
        for h in range(HG_HEADS):
            s = s_ref[bi, h] * dt[:, h:h + 1] + kt[:, h:h + 1] * v[h:h + 1, :]
            s_out_ref[bi, h] = s
            outs.append(jnp.sum(s * qt[:, h:h + 1], axis=0, keepdims=True))
        o = jnp.concatenate(outs, axis=0)
        y_ref[bi] = (_rms(o) * go * og_ref[bi].astype(F32)).astype(y_ref.dtype)
        return carry

    lax.fori_loop(0, q_ref.shape[0], one, 0)


def _gla_step(qs, k, v, og, lg, g_out_heads, state, layer, bb=8):
    rows = qs.shape[0]
    shp = (rows, HG_HEADS, HG_DK)
    stream = pl.BlockSpec((bb, HG_HEADS, HG_DK), lambda i: (i, 0, 0))
    return pl.pallas_call(
        _gla_step_body,
        out_shape=[jax.ShapeDtypeStruct(shp, F32),
                   jax.ShapeDtypeStruct((rows, HG_HEADS, HG_DK, HG_DV), F32)],
        grid=(rows // bb,),
        in_specs=[stream, stream, stream, stream, stream, _full_spec((HG_HEADS, HG_DV)),
                  pl.BlockSpec((None, bb, HG_HEADS, HG_DK, HG_DV), lambda i: (layer, i, 0, 0, 0))],
        out_specs=[stream, pl.BlockSpec((bb, HG_HEADS, HG_DK, HG_DV), lambda i: (i, 0, 0, 0))],
        compiler_params=_params("parallel"),
        name="gla_step",
    )(qs.reshape(shp), k.reshape(shp), v.reshape(shp), og.reshape(shp), lg.reshape(shp), g_out_heads, state)


def _proj_res_body(a_ref, w_ref, x_ref, gate_ref, o_ref):
    o_ref[...] = x_ref[...] + gate_ref[...] * _dot(a_ref[...].astype(BF16), w_ref[...])


def _proj_res(grp, a, w, layer, x, mods, sub):
    kdim = a.shape[1]
    return pl.pallas_call(
        _proj_res_body,
        out_shape=jax.ShapeDtypeStruct((grp.rows, D_MODEL), F32),
        grid=(grp.rows // grp.tm,),
        in_specs=[_row_spec(grp, kdim),
                  pl.BlockSpec((None, kdim, D_MODEL), lambda i: (layer, 0, 0)),
                  _row_spec(grp, D_MODEL),
                  _mod_spec(grp, mods, sub * 3 + 2)],
        out_specs=_row_spec(grp, D_MODEL),
        compiler_params=_params("parallel"),
        name="proj_residual",
    )(a, w, x, mods)


def _kv_body(x_ref, shift_ref, scale_ref, g_ref, wc_ref, wr_ref, wrr_ref, gl_ref, cos_ref, sin_ref,
             ckv_ref, kr_ref, kcat_ref, ckvt_ref):
    h = _mod_norm(x_ref[...], g_ref[...], scale_ref[...], shift_ref[...]).astype(BF16)
    ckv = _rms(_dot(h, wc_ref[...])) * gl_ref[...]
    kr = _dot(h, wr_ref[...]) * cos_ref[...] + _dot(h, wrr_ref[...]) * sin_ref[...]
    ckv_ref[...] = ckv
    kr_ref[...] = kr
    kcat_ref[:, 0:KV_RANK] = ckv.astype(BF16)
    kcat_ref[:, KV_RANK:QK_LAT] = kr.astype(BF16)
    ckvt_ref[0:KV_RANK, :] = ckv.T.astype(BF16)
    ckvt_ref[KV_RANK:VT_ROWS, :] = jnp.ones((VT_ROWS - KV_RANK, ckvt_ref.shape[1]), BF16)


def _shared_kv(grp, x, kvmods, g_norm_row, w_c, w_r, w_rr, g_latent_row, cos_t, sin_t):
    tps = grp.tiles_per_seq
    return pl.pallas_call(
        _kv_body,
        out_shape=[jax.ShapeDtypeStruct((grp.rows, KV_RANK), F32),
                   jax.ShapeDtypeStruct((grp.rows, QK_ROPE), F32),
                   jax.ShapeDtypeStruct((grp.rows, QK_LAT), BF16),
                   jax.ShapeDtypeStruct((grp.rows // grp.seq, VT_ROWS, grp.seq), BF16)],
        grid=(grp.rows // grp.tm,),
        in_specs=[_row_spec(grp, D_MODEL),
                  _mod_spec(grp, kvmods, 0),
                  _mod_spec(grp, kvmods, 1),
                  _full_spec((1, D_MODEL)),
                  _full_spec(w_c.shape), _full_spec(w_r.shape), _full_spec(w_rr.shape),
                  _full_spec((1, KV_RANK)),
                  _pos_spec(grp, QK_ROPE), _pos_spec(grp, QK_ROPE)],
        out_specs=[_row_spec(grp, KV_RANK), _row_spec(grp, QK_ROPE), _row_spec(grp, QK_LAT),
                   pl.BlockSpec((None, VT_ROWS, grp.tm), lambda i: (i // tps, 0, i % tps))],
        compiler_params=_params("parallel"),
        name="shared_kv",
    )(x, kvmods, kvmods, g_norm_row, w_c, w_r, w_rr, g_latent_row, cos_t, sin_t)


def _mla_q_body(x_ref, shift_ref, scale_ref, g_ref, wdq_ref, gq_ref, wn_ref, wrt_ref, wrrt_ref, wukt_ref,
                cos_ref, sin_ref, q_ref):
    tm = x_ref.shape[0]
    h = _mod_norm(x_ref[...], g_ref[...], scale_ref[...], shift_ref[...]).astype(BF16)
    qc = (_rms(_dot(h, wdq_ref[...])) * gq_ref[...]).astype(BF16)
    qn = _dot(qc, wn_ref[...])
    qrt = (_dot_nt(wrt_ref[...], qc) * cos_ref[...] + _dot_nt(wrrt_ref[...], qc) * sin_ref[...]) * SM_SCALE
    for hd in range(MLA_HEADS):
        cols = slice(hd * tm, (hd + 1) * tm)
        qlt = _dot_nt(wukt_ref[hd], qn[:, hd * QK_NOPE:(hd + 1) * QK_NOPE].astype(BF16)) * SM_SCALE
        q_ref[0:KV_RANK, cols] = qlt.astype(q_ref.dtype)
        q_ref[KV_RANK:QK_LAT, cols] = qrt[hd * QK_ROPE:(hd + 1) * QK_ROPE, :].astype(q_ref.dtype)


def _mla_q(grp, x, mods, g_norm_row, w_dq, g_q_row, w_n, w_rt, w_rrt, w_ukt, cos_t, sin_t, layer_b):
    sub = 1
    nseq = grp.rows // grp.seq
    tps = grp.tiles_per_seq
    width = MLA_HEADS * QK_ROPE
    sel = lambda a: pl.BlockSpec((None,) + a.shape[1:], lambda i: (layer_b,) + (0,) * (a.ndim - 1))
    if grp.per_row:
        pos = pl.BlockSpec((width, grp.tm), lambda i: (0, i))
    else:
        pos = pl.BlockSpec((width, grp.tm), lambda i: (0, i % tps))
    return pl.pallas_call(
        _mla_q_body,
        out_shape=jax.ShapeDtypeStruct((nseq, tps, QK_LAT, MLA_HEADS * grp.tm), BF16),
        grid=(grp.rows // grp.tm,),
        in_specs=[_row_spec(grp, D_MODEL),
                  _mod_spec(grp, mods, sub * 3 + 0),
                  _mod_spec(grp, mods, sub * 3 + 1),
                  _full_spec((1, D_MODEL)),
                  sel(w_dq), sel(g_q_row), sel(w_n), sel(w_rt), sel(w_rrt),
                  _full_spec(w_ukt.shape),
                  pos, pos],
        out_specs=pl.BlockSpec((None, None, QK_LAT, MLA_HEADS * grp.tm), lambda i: (i // tps, i % tps, 0, 0)),
        compiler_params=_params("parallel"),
        name="mla_q",
    )(x, mods, mods, g_norm_row, w_dq, g_q_row, w_n, w_rt, w_rrt, w_ukt, cos_t, sin_t)


def _attn_body(qi_ref, kj_ref, qt_ref, k_ref, vt_ref, o_ref, m_scr, acc_scr, *, tq):
    p = pl.program_id(1)
    i = qi_ref[p]
    j = kj_ref[p]

    @pl.when(j == 0)
    def _():
        m_scr[...] = jnp.full_like(m_scr, -jnp.inf)
        acc_scr[...] = jnp.zeros_like(acc_scr)

    def update(masked):
        s = _dot(k_ref[...], qt_ref[...])
        if masked:
            kpos = lax.broadcasted_iota(jnp.int32, s.shape, 0)
            qpos = lax.broadcasted_iota(jnp.int32, s.shape, 1) & (tq - 1)
            s = jnp.where(kpos <= qpos, s, -jnp.inf)
        m_prev = m_scr[...]
        m_new = jnp.maximum(m_prev, jnp.max(s, axis=0, keepdims=True))
        alpha = jnp.exp(m_prev - m_new)
        pr = jnp.exp(s - m_new).astype(BF16)
        acc_scr[...] = alpha * acc_scr[...] + _dot(vt_ref[...], pr)
        m_scr[...] = m_new

    @pl.when(j < i)
    def _():
        update(False)

    @pl.when(j == i)
    def _():
        update(True)
        out = acc_scr[0:KV_RANK, :] / acc_scr[KV_RANK:KV_RANK + 1, :]
        for hd in range(MLA_HEADS):
            o_ref[:, hd * KV_RANK:(hd + 1) * KV_RANK] = out[:, hd * tq:(hd + 1) * tq].T.astype(o_ref.dtype)


def _attend_prompt(qt, kcat, ckvt, batch, seq, tq):
    nq = seq // tq
    pairs = [(i, j) for i in range(nq) for j in range(i + 1)]
    qi = jnp.asarray(np.array([p[0] for p in pairs], np.int32))
    kj = jnp.asarray(np.array([p[1] for p in pairs], np.int32))
    cols = MLA_HEADS * tq
    return pl.pallas_call(
        functools.partial(_attn_body, tq=tq),
        out_shape=jax.ShapeDtypeStruct((batch, seq, MLA_HEADS * KV_RANK), BF16),
        grid_spec=pltpu.PrefetchScalarGridSpec(
            num_scalar_prefetch=2,
            grid=(batch, len(pairs)),
            in_specs=[pl.BlockSpec((None, None, QK_LAT, cols), lambda b, p, qi, kj: (b, qi[p], 0, 0)),
                      pl.BlockSpec((None, tq, QK_LAT), lambda b, p, qi, kj: (b, kj[p], 0)),
                      pl.BlockSpec((None, VT_ROWS, tq), lambda b, p, qi, kj: (b, 0, kj[p]))],
            out_specs=pl.BlockSpec((None, tq, MLA_HEADS * KV_RANK), lambda b, p, qi, kj: (b, qi[p], 0)),
            scratch_shapes=[pltpu.VMEM((1, cols), F32), pltpu.VMEM((VT_ROWS, cols), F32)],
        ),
        compiler_params=_params("parallel", "arbitrary"),
        name="attend_prompt",
    )(qi, kj, qt, kcat.reshape(batch, seq, QK_LAT), ckvt)


PAGED_CHUNK_PAGES = 32
PAGED_SLOTS = 4


def _paged_body(pt_ref, q_ref, cn_ref, kn_ref, ckv_hbm, krt_hbm, o_ref, cbuf, kbuf, sems, *, n_pages):
    ch = PAGED_CHUNK_PAGES
    nc = n_pages // ch
    ahead = PAGED_SLOTS - 1
    nseq = pl.num_programs(0)
    b = pl.program_id(0)

    def copies(g, slot):
        out = []
        for pg in range(ch):
            page = pt_ref[g * ch + pg]
            dst = pl.ds(pg * PAGE_SIZE, PAGE_SIZE)
            out.append(pltpu.make_async_copy(ckv_hbm.at[page], cbuf.at[slot, dst, :], sems.at[0, slot]))
            out.append(pltpu.make_async_copy(krt_hbm.at[page], kbuf.at[slot, pg], sems.at[1, slot]))
        return out

    def start(g, slot):
        for cp in copies(g, slot):
            cp.start()

    def wait(slot):
        for cp in copies(0, slot):
            cp.wait()

    @pl.when(b == 0)
    def _():
        for c in range(ahead):
            start(c, c)

    q = q_ref[...]
    ql = q[:, 0:KV_RANK]
    qr = q[:, KV_RANK:QK_LAT]
    cn = cn_ref[...]
    kn = kn_ref[...]
    m = jnp.sum(ql * cn, axis=1, keepdims=True) + jnp.sum(qr * kn, axis=1, keepdims=True)
    l = jnp.ones_like(m)
    acc = jnp.broadcast_to(cn, (MLA_HEADS, KV_RANK))
    qlb = ql.astype(BF16)
    qrb = qr.astype(BF16)
    for c in range(nc):
        slot = c % PAGED_SLOTS
        g = b * nc + c
        nxt = (c + ahead) % PAGED_SLOTS
        if c + ahead < nc:
            start(g + ahead, nxt)
        else:
            @pl.when(b + 1 < nseq)
            def _():
                start(g + ahead, nxt)
        wait(slot)
        kc = cbuf[slot].astype(BF16)
        s_rope = [_dot(qrb, kbuf[slot, pg].astype(BF16)) for pg in range(ch)]
        s = _dot_nt(qlb, kc) + jnp.concatenate(s_rope, axis=1)
        m_new = jnp.maximum(m, jnp.max(s, axis=1, keepdims=True))
        alpha = jnp.exp(m - m_new)
        pr = jnp.exp(s - m_new)
        l = alpha * l + jnp.sum(pr, axis=1, keepdims=True)
        acc = alpha * acc + _dot(pr.astype(BF16), kc)
        m = m_new
    o_ref[...] = acc / l


def _attend_paged(q, ckv_new, krope_new, cache_ckv, cache_krope_t, page_table):
    nseq, n_pages = page_table.shape
    ch = PAGED_CHUNK_PAGES
    assert n_pages % (PAGED_SLOTS * ch) == 0
    per_seq = lambda *tail: pl.BlockSpec((None,) + tail, lambda i, pt: (i,) + (0,) * len(tail))
    return pl.pallas_call(
        functools.partial(_paged_body, n_pages=n_pages),
        out_shape=jax.ShapeDtypeStruct((nseq, MLA_HEADS, KV_RANK), F32),
        grid_spec=pltpu.PrefetchScalarGridSpec(
            num_scalar_prefetch=1,
            grid=(nseq,),
            in_specs=[per_seq(MLA_HEADS, QK_LAT), per_seq(1, KV_RANK), per_seq(1, QK_ROPE),
                      pl.BlockSpec(memory_space=pl.ANY), pl.BlockSpec(memory_space=pl.ANY)],
            out_specs=per_seq(MLA_HEADS, KV_RANK),
            scratch_shapes=[pltpu.VMEM((PAGED_SLOTS, ch * PAGE_SIZE, KV_RANK), F32),
                            pltpu.VMEM((PAGED_SLOTS, ch, QK_ROPE, PAGE_SIZE), F32),
                            pltpu.SemaphoreType.DMA((2, PAGED_SLOTS))],
        ),
        compiler_params=_params("arbitrary"),
        name="attend_paged",
    )(page_table.reshape(-1), q, ckv_new.reshape(nseq, 1, KV_RANK), krope_new.reshape(nseq, 1, QK_ROPE),
      cache_ckv, cache_krope_t)


def _attn_out_body(o_ref, wuv_ref, wo_ref, x_ref, gate_ref, out_ref):
    parts = [_dot(o_ref[:, hd * KV_RANK:(hd + 1) * KV_RANK], wuv_ref[hd]).astype(BF16)
             for hd in range(MLA_HEADS)]
    o = jnp.concatenate(parts, axis=1)
    out_ref[...] = x_ref[...] + gate_ref[...] * _dot(o, wo_ref[...])


def _attn_out(grp, o_lat, w_uv_t, w_out, layer_b, x, mods, sub):
    width = MLA_HEADS * KV_RANK
    return pl.pallas_call(
        _attn_out_body,
        out_shape=jax.ShapeDtypeStruct((grp.rows, D_MODEL), F32),
        grid=(grp.rows // grp.tm,),
        in_specs=[_row_spec(grp, width),
                  _full_spec(w_uv_t.shape),
                  pl.BlockSpec((None, MLA_HEADS * V_HEAD, D_MODEL), lambda i: (layer_b, 0, 0)),
                  _row_spec(grp, D_MODEL),
                  _mod_spec(grp, mods, sub * 3 + 2)],
        out_specs=_row_spec(grp, D_MODEL),
        compiler_params=_params("parallel"),
        name="attn_out",
    )(o_lat, w_uv_t, w_out, x, mods)


def _rope_tables(pos):
    half = QK_ROPE // 2
    inv_freq = ROPE_THETA ** (-2.0 * jnp.arange(half, dtype=F32) / QK_ROPE)
    ang = pos.astype(F32)[:, None] * inv_freq[None, :]
    cos, sin = jnp.cos(ang), jnp.sin(ang)
    return jnp.concatenate([cos, cos], axis=-1), jnp.concatenate([-sin, sin], axis=-1)


def _swap_halves(w):
    half = QK_ROPE // 2
    return jnp.concatenate([w[..., half:], w[..., :half]], axis=-1)


def kernel(x_prompt, x_sample, c_prompt, c_sample, state_hgrn, cache_ckv, cache_krope, page_table, w_ada, b_ada, g_norm, w_ffn_gu, w_ffn_down, hg_w_in, hg_lb, hg_g_out, hg_w_out, kv_w_ada, kv_b_ada, kv_g_norm, kv_w_down, kv_g_latent, kv_w_uk, kv_w_uv, q_w_down, q_g_norm, q_w_up, attn_w_out, g_final):
    batch, seq, d = x_prompt.shape
    nseq = x_sample.shape[0]
    n_b = DEPTH - N_A_LAYERS

    w_gu = w_ffn_gu
    w_dn = w_ffn_down
    w_in = hg_w_in.astype(BF16)
    w_ho = hg_w_out.astype(BF16)
    w_ao = attn_w_out.astype(BF16)
    w_kc = kv_w_down[:, :KV_RANK].astype(BF16)
    w_kr = kv_w_down[:, KV_RANK:].astype(BF16)
    w_krr = _swap_halves(kv_w_down[:, KV_RANK:]).astype(BF16)
    w_dq = q_w_down.astype(BF16)
    w_qn = q_w_up[..., :QK_NOPE].reshape(n_b, Q_RANK, MLA_HEADS * QK_NOPE).astype(BF16)
    w_qr = q_w_up[..., QK_NOPE:].reshape(n_b, Q_RANK, MLA_HEADS * QK_ROPE)
    w_qrr = _swap_halves(q_w_up[..., QK_NOPE:]).reshape(n_b, Q_RANK, MLA_HEADS * QK_ROPE)
    w_qrt = w_qr.transpose(0, 2, 1).astype(BF16)
    w_qrrt = w_qrr.transpose(0, 2, 1).astype(BF16)
    w_ukt = kv_w_uk.transpose(1, 0, 2).astype(BF16)
    w_uv_t = kv_w_uv.transpose(1, 0, 2).astype(BF16)
    cache_krope_t = cache_krope.transpose(0, 2, 1)
    g_q_rows = q_g_norm.reshape(n_b, 1, Q_RANK)
    g_out_rows = hg_g_out.reshape(N_A_LAYERS, 1, D_MODEL)

    pad = 8
    c_all = jnp.concatenate([c_prompt, jnp.zeros((pad - batch, d), F32), c_sample], axis=0)
    mods_all = [_ada(c_all, w_ada, b_ada.reshape(DEPTH, 1, -1), l) for l in range(DEPTH)]
    kvm_all = _ada(c_all, kv_w_ada[None], kv_b_ada.reshape(1, 1, -1), 0)

    def trunk(grp, x, mods_l, kvmods, pos, mix, attend, stream_dtype):
        grp_half = grp._replace(tm=max(grp.tm // 2, 128))
        grp_q = grp._replace(tm=min(grp.tm, ATTN_TQ))
        cos_k, sin_k = _rope_tables(pos)
        cos_qt, sin_qt = jnp.tile(cos_k, (1, MLA_HEADS)).T, jnp.tile(sin_k, (1, MLA_HEADS)).T
        states = []
        ckv = krope = kcat = ckvt = None
        for l in range(DEPTH):
            mods = mods_l[l]
            gn = lambda j: g_norm[l, j].reshape(1, d)
            if l == N_A_LAYERS:
                ckv, krope, kcat, ckvt = _shared_kv(grp, x, kvmods, kv_g_norm.reshape(1, d), w_kc, w_kr, w_krr,
                                                    kv_g_latent.reshape(1, KV_RANK), cos_k, sin_k)
            x = _ffn(grp, x, mods, 0, gn(0), w_gu, w_dn, l, 0)
            if l < N_A_LAYERS:
                qs, k, v, og, lg = _hgrn_in(grp_half, x, mods, gn(1), w_in, hg_lb, l, stream_dtype)
                y, s = mix(l, qs, k, v, og, lg)
                states.append(s)
                x = _proj_res(grp, y, w_ho, l, x, mods, 1)
            else:
                lb_ = l - N_A_LAYERS
                qt = _mla_q(grp_q, x, mods, gn(1), w_dq, g_q_rows, w_qn, w_qrt, w_qrrt, w_ukt, cos_qt, sin_qt,
                            lb_)
                o_lat = attend(qt, ckv, krope, kcat, ckvt)
                x = _attn_out(grp, o_lat, w_uv_t, w_ao, lb_, x, mods, 1)
            x = _ffn(grp, x, mods, 2, gn(2), w_gu, w_dn, l, 1,
                     g_final=g_final.reshape(1, d) if l == DEPTH - 1 else None)
        return x, jnp.stack(states), ckv, krope

    grp_p = Group(rows=batch * seq, seq=seq, tm=1024, per_row=False)
    mods_p = [m[:pad].reshape(pad, 1, -1) for m in mods_all]
    kvm_p = kvm_all[:pad].reshape(pad, 1, -1)

    def mix_p(l, qs, k, v, og, lg):
        y, s = _gla_prompt(qs, k, v, og, lg, g_out_rows[l], batch, seq)
        return y.reshape(batch * seq, d), s

    def attend_p(qt, ckv, krope, kcat, ckvt):
        o = _attend_prompt(qt, kcat, ckvt, batch, seq, ATTN_TQ)
        return o.reshape(batch * seq, MLA_HEADS * KV_RANK)

    y_p, st_p, ckv_p, kr_p = trunk(grp_p, x_prompt.reshape(batch * seq, d), mods_p, kvm_p,
                                   jnp.arange(seq), mix_p, attend_p, BF16)

    grp_s = Group(rows=nseq, seq=nseq, tm=nseq, per_row=True)
    mods_s = [m[pad:] for m in mods_all]
    kvm_s = kvm_all[pad:]

    def mix_s(l, qs, k, v, og, lg):
        y, s = _gla_step(qs, k, v, og, lg, hg_g_out[l].reshape(HG_HEADS, HG_DV), state_hgrn, l)
        return y.reshape(nseq, d), s

    def attend_s(qt, ckv, krope, kcat, ckvt):
        qf = qt.reshape(QK_LAT, MLA_HEADS, nseq).transpose(2, 1, 0).astype(F32)
        o = _attend_paged(qf, ckv, krope, cache_ckv, cache_krope_t, page_table)
        return o.reshape(nseq, MLA_HEADS * KV_RANK).astype(BF16)

    y_s, st_s, ckv_s, kr_s = trunk(grp_s, x_sample.reshape(nseq, d), mods_s, kvm_s,
                                   jnp.full((nseq,), PAST_LEN), mix_s, attend_s, F32)

    return (y_p.reshape(batch, seq, d), y_s.reshape(nseq, 1, d),
            st_p.astype(state_hgrn.dtype), st_s.astype(state_hgrn.dtype),
            ckv_p.reshape(batch, seq, KV_RANK), kr_p.reshape(batch, seq, QK_ROPE),
            ckv_s.reshape(nseq, 1, KV_RANK), kr_s.reshape(nseq, 1, QK_ROPE))
```

```python
import functools
from typing import NamedTuple

import jax
import jax.numpy as jnp
import numpy as np
from jax import lax
from jax.experimental import pallas as pl
from jax.experimental.pallas import tpu as pltpu

D_MODEL = 1024
DEPTH = 2
N_A_LAYERS = DEPTH // 2
PAST_LEN = 16384
PAGE_SIZE = 128
D_FF = 2816
HG_HEADS = 8
HG_DK = D_MODEL // HG_HEADS
HG_DV = D_MODEL // HG_HEADS
MLA_HEADS = 8
QK_NOPE = 128
QK_ROPE = 64
V_HEAD = 128
KV_RANK = 256
Q_RANK = 384
QK_LAT = KV_RANK + QK_ROPE
ROPE_THETA = 10000.0
EPS = 1e-6
SM_SCALE = (QK_NOPE + QK_ROPE) ** -0.5

BF16 = jnp.bfloat16
F32 = jnp.float32

V7X_VMEM_BYTES = 64 * 1024 * 1024
VMEM_LIMIT = V7X_VMEM_BYTES * 3 // 4

GLA_CHUNK = 64
GLA_SUB = 16
GLA_BLOCK = 256
GLA_SAFE_DECAY = 60.0

ATTN_TQ = 512
VT_ROWS = KV_RANK + 16

NT_DIMS = (((1,), (1,)), ((), ()))


class Group(NamedTuple):
    rows: int
    seq: int
    tm: int
    per_row: bool

    @property
    def tiles_per_seq(self):
        return self.seq // self.tm


def _params(*sem):
    return pltpu.CompilerParams(dimension_semantics=sem, vmem_limit_bytes=VMEM_LIMIT)


def _dot(a, b):
    return jnp.dot(a, b, preferred_element_type=F32)


def _dot_nt(a, b):
    return lax.dot_general(a, b, NT_DIMS, preferred_element_type=F32)


def _sigmoid(x):
    return 1.0 / (1.0 + jnp.exp(-x))


def _rms(x):
    return x * lax.rsqrt(jnp.mean(x * x, axis=-1, keepdims=True) + EPS)


def _mod_norm(x, g, scale, shift):
    return (_rms(x) * g) * (1.0 + scale) + shift


def _mod_spec(grp, mods, col):
    if grp.per_row:
        return pl.BlockSpec((grp.tm, D_MODEL), lambda *g: (g[0], col))
    tps = grp.tiles_per_seq
    return pl.BlockSpec((None, 1, D_MODEL), lambda *g: (g[0] // tps, 0, col))


def _pos_spec(grp, width):
    if grp.per_row:
        return pl.BlockSpec((grp.tm, width), lambda *g: (g[0], 0))
    tps = grp.tiles_per_seq
    return pl.BlockSpec((grp.tm, width), lambda *g: (g[0] % tps, 0))


def _row_spec(grp, width):
    return pl.BlockSpec((grp.tm, width), lambda *g: (g[0], 0))


def _full_spec(shape):
    nd = len(shape)
    return pl.BlockSpec(shape, lambda *g: (0,) * nd)


def _ada_body(c_ref, w_ref, b_ref, o_ref):
    c = c_ref[...]
    sc = (c * _sigmoid(c)).astype(BF16)
    o_ref[...] = _dot(sc, w_ref[...].astype(BF16)) + b_ref[...]


def _ada(c_all, w, b, layer, tn=1024):
    rows = c_all.shape[0]
    n = w.shape[-1]
    return pl.pallas_call(
        _ada_body,
        out_shape=jax.ShapeDtypeStruct((rows, n), F32),
        grid=(n // tn,),
        in_specs=[
            _full_spec((rows, D_MODEL)),
            pl.BlockSpec((None, D_MODEL, tn), lambda j: (layer, 0, j)),
            pl.BlockSpec((None, 1, tn), lambda j: (layer, 0, j)),
        ],
        out_specs=pl.BlockSpec((rows, tn), lambda j: (0, j)),
        compiler_params=_params("arbitrary"),
        name="ada_proj",
    )(c_all, w, b)


FFN_TF = 256


def _ffn_body(x_ref, shift_ref, scale_ref, gate_ref, g_ref, wgu_hbm, wd_hbm, *rest, layer, half, final_norm):
    if final_norm:
        gf_ref, o_ref, h_scr, acc_scr, wg_buf, wu_buf, wd_buf, sems = rest
    else:
        o_ref, h_scr, acc_scr, wg_buf, wu_buf, wd_buf, sems = rest
    tf = FFN_TF
    nf = D_FF // tf
    i = pl.program_id(0)
    ni = pl.num_programs(0)

    def copies(j, slot):
        return [
            pltpu.make_async_copy(wgu_hbm.at[layer, half, :, pl.ds(j * tf, tf)], wg_buf.at[slot], sems.at[0, slot]),
            pltpu.make_async_copy(wgu_hbm.at[layer, half, :, pl.ds((nf + j) * tf, tf)], wu_buf.at[slot],
                                  sems.at[1, slot]),
            pltpu.make_async_copy(wd_hbm.at[layer, half, pl.ds(j * tf, tf), :], wd_buf.at[slot], sems.at[2, slot]),
        ]

    def start(j, slot):
        for cp in copies(j, slot):
            cp.start()

    def wait(j, slot):
        for cp in copies(j, slot):
            cp.wait()

    @pl.when(i == 0)
    def _():
        start(0, 0)

    h_scr[...] = _mod_norm(x_ref[...], g_ref[...], scale_ref[...], shift_ref[...]).astype(BF16)
    for j in range(nf):
        slot = (i * nf + j) % 2
        if j + 1 < nf:
            start(j + 1, 1 - slot)
        else:
            @pl.when(i + 1 < ni)
            def _():
                start(0, 1 - slot)
        wait(j, slot)
        h = h_scr[...]
        a = _dot(h, wg_buf[slot].astype(BF16))
        b = _dot(h, wu_buf[slot].astype(BF16))
        act = ((a * _sigmoid(a)) * b).astype(BF16)
        part = _dot(act, wd_buf[slot].astype(BF16))
        if j == 0:
            acc_scr[...] = part
        else:
            acc_scr[...] += part

    y = x_ref[...] + (0.5 * gate_ref[...]) * acc_scr[...]
    if final_norm:
        y = _rms(y) * gf_ref[...]
    o_ref[...] = y


def _ffn(grp, x, mods, sub, g_norm_row, w_gu, w_down, layer, half, g_final=None):
    tf = FFN_TF
    final_norm = g_final is not None
    in_specs = [
        _row_spec(grp, D_MODEL),
        _mod_spec(grp, mods, sub * 3 + 0),
        _mod_spec(grp, mods, sub * 3 + 1),
        _mod_spec(grp, mods, sub * 3 + 2),
        _full_spec((1, D_MODEL)),
        pl.BlockSpec(memory_space=pl.ANY),
        pl.BlockSpec(memory_space=pl.ANY),
    ]
    args = [x, mods, mods, mods, g_norm_row, w_gu, w_down]
    if final_norm:
        in_specs.append(_full_spec((1, D_MODEL)))
        args.append(g_final)
    return pl.pallas_call(
        functools.partial(_ffn_body, layer=layer, half=half, final_norm=final_norm),
        out_shape=jax.ShapeDtypeStruct((grp.rows, D_MODEL), F32),
        grid=(grp.rows // grp.tm,),
        in_specs=in_specs,
        out_specs=_row_spec(grp, D_MODEL),
        scratch_shapes=[pltpu.VMEM((grp.tm, D_MODEL), BF16), pltpu.VMEM((grp.tm, D_MODEL), F32),
                        pltpu.VMEM((2, D_MODEL, tf), F32), pltpu.VMEM((2, D_MODEL, tf), F32),
                        pltpu.VMEM((2, tf, D_MODEL), F32), pltpu.SemaphoreType.DMA((3, 2))],
        compiler_params=_params("arbitrary"),
        name="ffn_half",
    )(*args)


def _hgrn_in_body(x_ref, shift_ref, scale_ref, g_ref, w_ref, lb_ref, qs_ref, k_ref, v_ref, og_ref, lg_ref,
                  *, layer):
    d = D_MODEL
    h = _mod_norm(x_ref[...], g_ref[...], scale_ref[...], shift_ref[...]).astype(BF16)
    lbp = lb_ref[...]
    e = jnp.exp(lbp - jnp.max(lbp, axis=0, keepdims=True))
    lb = jnp.sum(e[: layer + 1], axis=0, keepdims=True) / jnp.sum(e, axis=0, keepdims=True)

    q = _dot(h, w_ref[:, 0:d])
    qs_ref[...] = (q * _sigmoid(q)).astype(qs_ref.dtype)
    f = _dot(h, w_ref[:, d:2 * d])
    log_sig = jnp.minimum(f, 0.0) - jnp.log(1.0 + jnp.exp(-jnp.abs(f)))
    a = jnp.log(lb)
    b = jnp.log(1.0 - lb) + log_sig
    lg_ref[...] = jnp.maximum(a, b) + jnp.log(1.0 + jnp.exp(-jnp.abs(a - b)))
    k_ref[...] = ((1.0 - lb) / (1.0 + jnp.exp(f))).astype(k_ref.dtype)
    v_ref[...] = _dot(h, w_ref[:, 2 * d:3 * d]).astype(v_ref.dtype)
    g = _dot(h, w_ref[:, 3 * d:4 * d])
    og_ref[...] = (g * _sigmoid(g)).astype(og_ref.dtype)


def _hgrn_in(grp, x, mods, g_norm_row, w_in, hg_lb, layer, stream_dtype):
    sub = 1
    outs = ([jax.ShapeDtypeStruct((grp.rows, D_MODEL), stream_dtype)] * 4
            + [jax.ShapeDtypeStruct((grp.rows, D_MODEL), F32)])
    return pl.pallas_call(
        functools.partial(_hgrn_in_body, layer=layer),
        out_shape=outs,
        grid=(grp.rows // grp.tm,),
        in_specs=[
            _row_spec(grp, D_MODEL),
            _mod_spec(grp, mods, sub * 3 + 0),
            _mod_spec(grp, mods, sub * 3 + 1),
            _full_spec((1, D_MODEL)),
            pl.BlockSpec((None, D_MODEL, 4 * D_MODEL), lambda i: (layer, 0, 0)),
            _full_spec(hg_lb.shape),
        ],
        out_specs=[_row_spec(grp, D_MODEL)] * 5,
        compiler_params=_params("parallel"),
        name="hgrn_in",
    )(x, mods, mods, g_norm_row, w_in, hg_lb)


def _segmented_cumsum(x, seg):
    row = lax.broadcasted_iota(jnp.int32, x.shape, 0) & (seg - 1)
    s = 1
    while s < seg:
        x = x + jnp.where(row >= s, pltpu.roll(x, s, 0), 0.0)
        s *= 2
    return x


def _gla_body(q_ref, k_ref, v_ref, og_ref, lg_ref, go_ref, y_ref, s_out_ref, st_scr, b_scr):
    c, sub = GLA_CHUNK, GLA_SUB
    t = pl.program_id(1)

    @pl.when(t == 0)
    def _():
        st_scr[...] = jnp.zeros_like(st_scr)

    b_scr[...] = _segmented_cumsum(lg_ref[...], c)
    tri = lax.broadcasted_iota(jnp.int32, (c, c), 1) <= lax.broadcasted_iota(jnp.int32, (c, c), 0)

    def finish(r0, h, o):
        cols = slice(h * HG_DV, (h + 1) * HG_DV)
        y = _rms(o) * go_ref[:, cols] * og_ref[pl.ds(r0, c), cols].astype(F32)
        y_ref[pl.ds(r0, c), cols] = y.astype(y_ref.dtype)

    def chunk(ci, carry):
        r0 = pl.multiple_of(ci * c, c)
        b_all = b_scr[pl.ds(r0, c), :]
        ref_rows = [jnp.zeros((1, D_MODEL), F32)] + [b_all[i * sub - 1:i * sub, :] for i in range(1, c // sub)]
        worst = functools.reduce(jnp.maximum, [ref_rows[i] - b_all[(i + 1) * sub - 1:(i + 1) * sub, :]
                                               for i in range(c // sub)])
        safe = jnp.max(worst) <= GLA_SAFE_DECAY

        @pl.when(safe)
        def _():
            for h in range(HG_HEADS):
                cols = slice(h * HG_DK, (h + 1) * HG_DK)
                b = b_all[:, cols]
                b_last = b[c - 1:c, :]
                refs = [r[:, cols] for r in ref_rows]
                q = q_ref[pl.ds(r0, c), cols].astype(F32)
                k = k_ref[pl.ds(r0, c), cols].astype(F32)
                v = v_ref[pl.ds(r0, c), cols]
                st = st_scr[h]
                o = _dot_nt((q * jnp.exp(b)).astype(BF16), st.astype(BF16))
                rows = []
                for i in range(c // sub):
                    qi = (q[i * sub:(i + 1) * sub] * jnp.exp(b[i * sub:(i + 1) * sub] - refs[i])).astype(BF16)
                    ki = (k * jnp.exp(jnp.minimum(refs[i] - b, GLA_SAFE_DECAY))).astype(BF16)
                    rows.append(_dot_nt(qi, ki))
                att = jnp.where(tri, jnp.concatenate(rows, axis=0), 0.0)
                o = o + _dot(att.astype(BF16), v)
                kd = (k * jnp.exp(b_last - b)).astype(BF16)
                vt = v.astype(F32).T.astype(BF16)
                st_scr[h] = st * jnp.exp(b_last) + _dot(vt, kd)
                finish(r0, h, o)

        @pl.when(jnp.logical_not(safe))
        def _():
            lane = lax.broadcasted_iota(jnp.int32, (HG_DV, c), 1)
            sublane = lax.broadcasted_iota(jnp.int32, (c, HG_DK), 0)
            for h in range(HG_HEADS):
                cols = slice(h * HG_DK, (h + 1) * HG_DK)
                vt = v_ref[pl.ds(r0, c), cols].astype(F32).T
                qf = q_ref[pl.ds(r0, c), cols].astype(F32)
                kf = k_ref[pl.ds(r0, c), cols].astype(F32)
                df = jnp.exp(lg_ref[pl.ds(r0, c), cols])

                def token(ti, carry2, vt=vt, qf=qf, kf=kf, df=df):
                    st, ot = carry2
                    sel = lane == ti
                    pick = lambda a: jnp.sum(jnp.where(sublane == ti, a, 0.0), axis=0, keepdims=True)
                    v_col = jnp.sum(jnp.where(sel, vt, 0.0), axis=1, keepdims=True)
                    st = st * pick(df) + v_col * pick(kf)
                    o_col = jnp.sum(st * pick(qf), axis=1, keepdims=True)
                    return st, jnp.where(sel, o_col, ot)

                st, ot = lax.fori_loop(0, c, token, (st_scr[h], jnp.zeros((HG_DV, c), F32)))
                st_scr[h] = st
                finish(r0, h, ot.T)

        return carry

    lax.fori_loop(0, q_ref.shape[0] // c, chunk, 0)

    @pl.when(t == pl.num_programs(1) - 1)
    def _():
        for h in range(HG_HEADS):
            s_out_ref[h] = st_scr[h].T


def _gla_prompt(qs, k, v, og, lg, g_out_row, batch, seq):
    tb = GLA_BLOCK
    stream = pl.BlockSpec((None, tb, D_MODEL), lambda b, t: (b, t, 0))
    shp = (batch, seq, D_MODEL)
    return pl.pallas_call(
        _gla_body,
        out_shape=[jax.ShapeDtypeStruct(shp, BF16),
                   jax.ShapeDtypeStruct((batch, HG_HEADS, HG_DK, HG_DV), F32)],
        grid=(batch, seq // tb),
        in_specs=[stream, stream, stream, stream, stream, _full_spec((1, D_MODEL))],
        out_specs=[stream, pl.BlockSpec((None, HG_HEADS, HG_DK, HG_DV), lambda b, t: (b, 0, 0, 0))],
        scratch_shapes=[pltpu.VMEM((HG_HEADS, HG_DV, HG_DK), F32), pltpu.VMEM((tb, D_MODEL), F32)],
        compiler_params=_params("parallel", "arbitrary"),
        name="gla_chunked",
    )(qs.reshape(shp), k.reshape(shp), v.reshape(shp), og.reshape(shp), lg.reshape(shp), g_out_row)


def _gla_step_body(q_ref, k_ref, v_ref, og_ref, lg_ref, go_ref, s_ref, y_ref, s_out_ref):
    go = go_ref[...]

    def one(bi, carry):
        qt = q_ref[bi].astype(F32).T
        kt = k_ref[bi].astype(F32).T
        dt = jnp.exp(lg_ref[bi]).T
        v = v_ref[bi].astype(F32)
        outs = []
        for h in range(HG_HEADS):
            s = s_ref[bi, h] * dt[:, h:h + 1] + kt[:, h:h + 1] * v[h:h + 1, :]
            s_out_ref[bi, h] = s
            outs.append(jnp.sum(s * qt[:, h:h + 1], axis=0, keepdims=True))
        o = jnp.concatenate(outs, axis=0)
        y_ref[bi] = (_rms(o) * go * og_ref[bi].astype(F32)).astype(y_ref.dtype)
        return carry

    lax.fori_loop(0, q_ref.shape[0], one, 0)


def _gla_step(qs, k, v, og, lg, g_out_heads, state, layer, bb=8):
    rows = qs.shape[0]
    shp = (rows, HG_HEADS, HG_DK)
    stream = pl.BlockSpec((bb, HG_HEADS, HG_DK), lambda i: (i, 0, 0))
    return pl.pallas_call(
        _gla_step_body,
        out_shape=[jax.ShapeDtypeStruct(shp, F32),
                   jax.ShapeDtypeStruct((rows, HG_HEADS, HG_DK, HG_DV), F32)],
        grid=(rows // bb,),
        in_specs=[stream, stream, stream, stream, stream, _full_spec((HG_HEADS, HG_DV)),
                  pl.BlockSpec((None, bb, HG_HEADS, HG_DK, HG_DV), lambda i: (layer, i, 0, 0, 0))],
        out_specs=[stream, pl.BlockSpec((bb, HG_HEADS, HG_DK, HG_DV), lambda i: (i, 0, 0, 0))],
        compiler_params=_params("parallel"),
        name="gla_step",
    )(qs.reshape(shp), k.reshape(shp), v.reshape(shp), og.reshape(shp), lg.reshape(shp), g_out_heads, state)


def _proj_res_body(a_ref, w_ref, x_ref, gate_ref, o_ref):
    o_ref[...] = x_ref[...] + gate_ref[...] * _dot(a_ref[...].astype(BF16), w_ref[...])


def _proj_res(grp, a, w, layer, x, mods, sub):
    kdim = a.shape[1]
    return pl.pallas_call(
        _proj_res_body,
        out_shape=jax.ShapeDtypeStruct((grp.rows, D_MODEL), F32),
        grid=(grp.rows // grp.tm,),
        in_specs=[_row_spec(grp, kdim),
                  pl.BlockSpec((None, kdim, D_MODEL), lambda i: (layer, 0, 0)),
                  _row_spec(grp, D_MODEL),
                  _mod_spec(grp, mods, sub * 3 + 2)],
        out_specs=_row_spec(grp, D_MODEL),
        compiler_params=_params("parallel"),
        name="proj_residual",
    )(a, w, x, mods)


def _kv_body(x_ref, shift_ref, scale_ref, g_ref, wc_ref, wr_ref, wrr_ref, gl_ref, cos_ref, sin_ref,
             ckv_ref, kr_ref, kcat_ref, ckvt_ref):
    h = _mod_norm(x_ref[...], g_ref[...], scale_ref[...], shift_ref[...]).astype(BF16)
    ckv = _rms(_dot(h, wc_ref[...])) * gl_ref[...]
    kr = _dot(h, wr_ref[...]) * cos_ref[...] + _dot(h, wrr_ref[...]) * sin_ref[...]
    ckv_ref[...] = ckv
    kr_ref[...] = kr
    kcat_ref[:, 0:KV_RANK] = ckv.astype(BF16)
    kcat_ref[:, KV_RANK:QK_LAT] = kr.astype(BF16)
    ckvt_ref[0:KV_RANK, :] = ckv.T.astype(BF16)
    ckvt_ref[KV_RANK:VT_ROWS, :] = jnp.ones((VT_ROWS - KV_RANK, ckvt_ref.shape[1]), BF16)


def _shared_kv(grp, x, kvmods, g_norm_row, w_c, w_r, w_rr, g_latent_row, cos_t, sin_t):
    tps = grp.tiles_per_seq
    return pl.pallas_call(
        _kv_body,
        out_shape=[jax.ShapeDtypeStruct((grp.rows, KV_RANK), F32),
                   jax.ShapeDtypeStruct((grp.rows, QK_ROPE), F32),
                   jax.ShapeDtypeStruct((grp.rows, QK_LAT), BF16),
                   jax.ShapeDtypeStruct((grp.rows // grp.seq, VT_ROWS, grp.seq), BF16)],
        grid=(grp.rows // grp.tm,),
        in_specs=[_row_spec(grp, D_MODEL),
                  _mod_spec(grp, kvmods, 0),
                  _mod_spec(grp, kvmods, 1),
                  _full_spec((1, D_MODEL)),
                  _full_spec(w_c.shape), _full_spec(w_r.shape), _full_spec(w_rr.shape),
                  _full_spec((1, KV_RANK)),
                  _pos_spec(grp, QK_ROPE), _pos_spec(grp, QK_ROPE)],
        out_specs=[_row_spec(grp, KV_RANK), _row_spec(grp, QK_ROPE), _row_spec(grp, QK_LAT),
                   pl.BlockSpec((None, VT_ROWS, grp.tm), lambda i: (i // tps, 0, i % tps))],
        compiler_params=_params("parallel"),
        name="shared_kv",
    )(x, kvmods, kvmods, g_norm_row, w_c, w_r, w_rr, g_latent_row, cos_t, sin_t)


def _mla_q_body(x_ref, shift_ref, scale_ref, g_ref, wdq_ref, gq_ref, wn_ref, wrt_ref, wrrt_ref, wukt_ref,
                cos_ref, sin_ref, q_ref):
    tm = x_ref.shape[0]
    h = _mod_norm(x_ref[...], g_ref[...], scale_ref[...], shift_ref[...]).astype(BF16)
    qc = (_rms(_dot(h, wdq_ref[...])) * gq_ref[...]).astype(BF16)
    qn = _dot(qc, wn_ref[...])
    qrt = (_dot_nt(wrt_ref[...], qc) * cos_ref[...] + _dot_nt(wrrt_ref[...], qc) * sin_ref[...]) * SM_SCALE
    for hd in range(MLA_HEADS):
        cols = slice(hd * tm, (hd + 1) * tm)
        qlt = _dot_nt(wukt_ref[hd], qn[:, hd * QK_NOPE:(hd + 1) * QK_NOPE].astype(BF16)) * SM_SCALE
        q_ref[0:KV_RANK, cols] = qlt.astype(q_ref.dtype)
        q_ref[KV_RANK:QK_LAT, cols] = qrt[hd * QK_ROPE:(hd + 1) * QK_ROPE, :].astype(q_ref.dtype)


def _mla_q(grp, x, mods, g_norm_row, w_dq, g_q_row, w_n, w_rt, w_rrt, w_ukt, cos_t, sin_t, layer_b):
    sub = 1
    nseq = grp.rows // grp.seq
    tps = grp.tiles_per_seq
    width = MLA_HEADS * QK_ROPE
    sel = lambda a: pl.BlockSpec((None,) + a.shape[1:], lambda i: (layer_b,) + (0,) * (a.ndim - 1))
    if grp.per_row:
        pos = pl.BlockSpec((width, grp.tm), lambda i: (0, i))
    else:
        pos = pl.BlockSpec((width, grp.tm), lambda i: (0, i % tps))
    return pl.pallas_call(
        _mla_q_body,
        out_shape=jax.ShapeDtypeStruct((nseq, tps, QK_LAT, MLA_HEADS * grp.tm), BF16),
        grid=(grp.rows // grp.tm,),
        in_specs=[_row_spec(grp, D_MODEL),
                  _mod_spec(grp, mods, sub * 3 + 0),
                  _mod_spec(grp, mods, sub * 3 + 1),
                  _full_spec((1, D_MODEL)),
                  sel(w_dq), sel(g_q_row), sel(w_n), sel(w_rt), sel(w_rrt),
                  _full_spec(w_ukt.shape),
                  pos, pos],
        out_specs=pl.BlockSpec((None, None, QK_LAT, MLA_HEADS * grp.tm), lambda i: (i // tps, i % tps, 0, 0)),
        compiler_params=_params("parallel"),
        name="mla_q",
    )(x, mods, mods, g_norm_row, w_dq, g_q_row, w_n, w_rt, w_rrt, w_ukt, cos_t, sin_t)


def _attn_body(qi_ref, kj_ref, qt_ref, k_ref, vt_ref, o_ref, m_scr, acc_scr, *, tq):
    p = pl.program_id(1)
    i = qi_ref[p]
    j = kj_ref[p]

    @pl.when(j == 0)
    def _():
        m_scr[...] = jnp.full_like(m_scr, -jnp.inf)
        acc_scr[...] = jnp.zeros_like(acc_scr)

    def update(masked):
        s = _dot(k_ref[...], qt_ref[...])
        if masked:
            kpos = lax.broadcasted_iota(jnp.int32, s.shape, 0)
            qpos = lax.broadcasted_iota(jnp.int32, s.shape, 1) & (tq - 1)
            s = jnp.where(kpos <= qpos, s, -jnp.inf)
        m_prev = m_scr[...]
        m_new = jnp.maximum(m_prev, jnp.max(s, axis=0, keepdims=True))
        alpha = jnp.exp(m_prev - m_new)
        pr = jnp.exp(s - m_new).astype(BF16)
        acc_scr[...] = alpha * acc_scr[...] + _dot(vt_ref[...], pr)
        m_scr[...] = m_new

    @pl.when(j < i)
    def _():
        update(False)

    @pl.when(j == i)
    def _():
        update(True)
        out = acc_scr[0:KV_RANK, :] / acc_scr[KV_RANK:KV_RANK + 1, :]
        for hd in range(MLA_HEADS):
            o_ref[:, hd * KV_RANK:(hd + 1) * KV_RANK] = out[:, hd * tq:(hd + 1) * tq].T.astype(o_ref.dtype)


def _attend_prompt(qt, kcat, ckvt, batch, seq, tq):
    nq = seq // tq
    pairs = [(i, j) for i in range(nq) for j in range(i + 1)]
    qi = jnp.asarray(np.array([p[0] for p in pairs], np.int32))
    kj = jnp.asarray(np.array([p[1] for p in pairs], np.int32))
    cols = MLA_HEADS * tq
    return pl.pallas_call(
        functools.partial(_attn_body, tq=tq),
        out_shape=jax.ShapeDtypeStruct((batch, seq, MLA_HEADS * KV_RANK), BF16),
        grid_spec=pltpu.PrefetchScalarGridSpec(
            num_scalar_prefetch=2,
            grid=(batch, len(pairs)),
            in_specs=[pl.BlockSpec((None, None, QK_LAT, cols), lambda b, p, qi, kj: (b, qi[p], 0, 0)),
                      pl.BlockSpec((None, tq, QK_LAT), lambda b, p, qi, kj: (b, kj[p], 0)),
                      pl.BlockSpec((None, VT_ROWS, tq), lambda b, p, qi, kj: (b, 0, kj[p]))],
            out_specs=pl.BlockSpec((None, tq, MLA_HEADS * KV_RANK), lambda b, p, qi, kj: (b, qi[p], 0)),
            scratch_shapes=[pltpu.VMEM((1, cols), F32), pltpu.VMEM((VT_ROWS, cols), F32)],
        ),
        compiler_params=_params("parallel", "arbitrary"),
        name="attend_prompt",
    )(qi, kj, qt, kcat.reshape(batch, seq, QK_LAT), ckvt)


PAGED_CHUNK_PAGES = 32
PAGED_SLOTS = 4


def _paged_body(pt_ref, q_ref, cn_ref, kn_ref, ckv_hbm, krt_hbm, o_ref, cbuf, kbuf, sems, *, n_pages):
    ch = PAGED_CHUNK_PAGES
    nc = n_pages // ch
    ahead = PAGED_SLOTS - 1
    nseq = pl.num_programs(0)
    b = pl.program_id(0)

    def copies(g, slot):
        out = []
        for pg in range(ch):
            page = pt_ref[g * ch + pg]
            dst = pl.ds(pg * PAGE_SIZE, PAGE_SIZE)
            out.append(pltpu.make_async_copy(ckv_hbm.at[page], cbuf.at[slot, dst, :], sems.at[0, slot]))
            out.append(pltpu.make_async_copy(krt_hbm.at[page], kbuf.at[slot, pg], sems.at[1, slot]))
        return out

    def start(g, slot):
        for cp in copies(g, slot):
            cp.start()

    def wait(slot):
        for cp in copies(0, slot):
            cp.wait()

    @pl.when(b == 0)
    def _():
        for c in range(ahead):
            start(c, c)

    q = q_ref[...]
    ql = q[:, 0:KV_RANK]
    qr = q[:, KV_RANK:QK_LAT]
    cn = cn_ref[...]
    kn = kn_ref[...]
    m = jnp.sum(ql * cn, axis=1, keepdims=True) + jnp.sum(qr * kn, axis=1, keepdims=True)
    l = jnp.ones_like(m)
    acc = jnp.broadcast_to(cn, (MLA_HEADS, KV_RANK))
    qlb = ql.astype(BF16)
    qrb = qr.astype(BF16)
    for c in range(nc):
        slot = c % PAGED_SLOTS
        g = b * nc + c
        nxt = (c + ahead) % PAGED_SLOTS
        if c + ahead < nc:
            start(g + ahead, nxt)
        else:
            @pl.when(b + 1 < nseq)
            def _():
                start(g + ahead, nxt)
        wait(slot)
        kc = cbuf[slot].astype(BF16)
        s_rope = [_dot(qrb, kbuf[slot, pg].astype(BF16)) for pg in range(ch)]
        s = _dot_nt(qlb, kc) + jnp.concatenate(s_rope, axis=1)
        m_new = jnp.maximum(m, jnp.max(s, axis=1, keepdims=True))
        alpha = jnp.exp(m - m_new)
        pr = jnp.exp(s - m_new)
        l = alpha * l + jnp.sum(pr, axis=1, keepdims=True)
        acc = alpha * acc + _dot(pr.astype(BF16), kc)
        m = m_new
    o_ref[...] = acc / l


def _attend_paged(q, ckv_new, krope_new, cache_ckv, cache_krope_t, page_table):
    nseq, n_pages = page_table.shape
    ch = PAGED_CHUNK_PAGES
    assert n_pages % (PAGED_SLOTS * ch) == 0
    per_seq = lambda *tail: pl.BlockSpec((None,) + tail, lambda i, pt: (i,) + (0,) * len(tail))
    return pl.pallas_call(
        functools.partial(_paged_body, n_pages=n_pages),
        out_shape=jax.ShapeDtypeStruct((nseq, MLA_HEADS, KV_RANK), F32),
        grid_spec=pltpu.PrefetchScalarGridSpec(
            num_scalar_prefetch=1,
            grid=(nseq,),
            in_specs=[per_seq(MLA_HEADS, QK_LAT), per_seq(1, KV_RANK), per_seq(1, QK_ROPE),
                      pl.BlockSpec(memory_space=pl.ANY), pl.BlockSpec(memory_space=pl.ANY)],
            out_specs=per_seq(MLA_HEADS, KV_RANK),
            scratch_shapes=[pltpu.VMEM((PAGED_SLOTS, ch * PAGE_SIZE, KV_RANK), F32),
                            pltpu.VMEM((PAGED_SLOTS, ch, QK_ROPE, PAGE_SIZE), F32),
                            pltpu.SemaphoreType.DMA((2, PAGED_SLOTS))],
        ),
        compiler_params=_params("arbitrary"),
        name="attend_paged",
    )(page_table.reshape(-1), q, ckv_new.reshape(nseq, 1, KV_RANK), krope_new.reshape(nseq, 1, QK_ROPE),
      cache_ckv, cache_krope_t)


def _attn_out_body(o_ref, wuv_ref, wo_ref, x_ref, gate_ref, out_ref):
    parts = [_dot(o_ref[:, hd * KV_RANK:(hd + 1) * KV_RANK], wuv_ref[hd]).astype(BF16)
             for hd in range(MLA_HEADS)]
    o = jnp.concatenate(parts, axis=1)
    out_ref[...] = x_ref[...] + gate_ref[...] * _dot(o, wo_ref[...])


def _attn_out(grp, o_lat, w_uv_t, w_out, layer_b, x, mods, sub):
    width = MLA_HEADS * KV_RANK
    return pl.pallas_call(
        _attn_out_body,
        out_shape=jax.ShapeDtypeStruct((grp.rows, D_MODEL), F32),
        grid=(grp.rows // grp.tm,),
        in_specs=[_row_spec(grp, width),
                  _full_spec(w_uv_t.shape),
                  pl.BlockSpec((None, MLA_HEADS * V_HEAD, D_MODEL), lambda i: (layer_b, 0, 0)),
                  _row_spec(grp, D_MODEL),
                  _mod_spec(grp, mods, sub * 3 + 2)],
        out_specs=_row_spec(grp, D_MODEL),
        compiler_params=_params("parallel"),
        name="attn_out",
    )(o_lat, w_uv_t, w_out, x, mods)


def _rope_tables(pos):
    half = QK_ROPE // 2
    inv_freq = ROPE_THETA ** (-2.0 * jnp.arange(half, dtype=F32) / QK_ROPE)
    ang = pos.astype(F32)[:, None] * inv_freq[None, :]
    cos, sin = jnp.cos(ang), jnp.sin(ang)
    return jnp.concatenate([cos, cos], axis=-1), jnp.concatenate([-sin, sin], axis=-1)


def _swap_halves(w):
    half = QK_ROPE // 2
    return jnp.concatenate([w[..., half:], w[..., :half]], axis=-1)


def kernel(x_prompt, x_sample, c_prompt, c_sample, state_hgrn, cache_ckv, cache_krope, page_table, w_ada, b_ada, g_norm, w_ffn_gu, w_ffn_down, hg_w_in, hg_lb, hg_g_out, hg_w_out, kv_w_ada, kv_b_ada, kv_g_norm, kv_w_down, kv_g_latent, kv_w_uk, kv_w_uv, q_w_down, q_g_norm, q_w_up, attn_w_out, g_final):
    batch, seq, d = x_prompt.shape
    nseq = x_sample.shape[0]
    n_b = DEPTH - N_A_LAYERS

    w_in = hg_w_in.astype(BF16)
    w_ho = hg_w_out.astype(BF16)
    w_ao = attn_w_out.astype(BF16)
    w_kc = kv_w_down[:, :KV_RANK].astype(BF16)
    w_kr = kv_w_down[:, KV_RANK:].astype(BF16)
    w_krr = _swap_halves(kv_w_down[:, KV_RANK:]).astype(BF16)
    w_dq = q_w_down.astype(BF16)
    w_qn = q_w_up[..., :QK_NOPE].reshape(n_b, Q_RANK, MLA_HEADS * QK_NOPE).astype(BF16)
    w_qr = q_w_up[..., QK_NOPE:].reshape(n_b, Q_RANK, MLA_HEADS * QK_ROPE)
    w_qrr = _swap_halves(q_w_up[..., QK_NOPE:]).reshape(n_b, Q_RANK, MLA_HEADS * QK_ROPE)
    w_qrt = w_qr.transpose(0, 2, 1).astype(BF16)
    w_qrrt = w_qrr.transpose(0, 2, 1).astype(BF16)
    w_ukt = kv_w_uk.transpose(1, 0, 2).astype(BF16)
    w_uv_t = kv_w_uv.transpose(1, 0, 2).astype(BF16)
    cache_krope_t = cache_krope.transpose(0, 2, 1)
    g_q_rows = q_g_norm.reshape(n_b, 1, Q_RANK)
    g_out_rows = hg_g_out.reshape(N_A_LAYERS, 1, D_MODEL)

    pad = 8
    c_all = jnp.concatenate([c_prompt, jnp.zeros((pad - batch, d), F32), c_sample], axis=0)
    mods_all = [_ada(c_all, w_ada, b_ada.reshape(DEPTH, 1, -1), l) for l in range(DEPTH)]
    kvm_all = _ada(c_all, kv_w_ada[None], kv_b_ada.reshape(1, 1, -1), 0)

    def trunk(grp, x, mods_l, kvmods, pos, mix, attend, stream_dtype):
        grp_half = grp._replace(tm=max(grp.tm // 2, 128))
        grp_q = grp._replace(tm=min(grp.tm, ATTN_TQ))
        cos_k, sin_k = _rope_tables(pos)
        cos_qt, sin_qt = jnp.tile(cos_k, (1, MLA_HEADS)).T, jnp.tile(sin_k, (1, MLA_HEADS)).T
        states = []
        ckv = krope = kcat = ckvt = None
        for l in range(DEPTH):
            mods = mods_l[l]
            gn = lambda j: g_norm[l, j].reshape(1, d)
            if l == N_A_LAYERS:
                ckv, krope, kcat, ckvt = _shared_kv(grp, x, kvmods, kv_g_norm.reshape(1, d), w_kc, w_kr, w_krr,
                                                    kv_g_latent.reshape(1, KV_RANK), cos_k, sin_k)
            x = _ffn(grp, x, mods, 0, gn(0), w_ffn_gu, w_ffn_down, l, 0)
            if l < N_A_LAYERS:
                qs, k, v, og, lg = _hgrn_in(grp_half, x, mods, gn(1), w_in, hg_lb, l, stream_dtype)
                y, s = mix(l, qs, k, v, og, lg)
                states.append(s)
                x = _proj_res(grp, y, w_ho, l, x, mods, 1)
            else:
                lb_ = l - N_A_LAYERS
                qt = _mla_q(grp_q, x, mods, gn(1), w_dq, g_q_rows, w_qn, w_qrt, w_qrrt, w_ukt, cos_qt, sin_qt,
                            lb_)
                o_lat = attend(qt, ckv, krope, kcat, ckvt)
                x = _attn_out(grp, o_lat, w_uv_t, w_ao, lb_, x, mods, 1)
            x = _ffn(grp, x, mods, 2, gn(2), w_ffn_gu, w_ffn_down, l, 1,
                     g_final=g_final.reshape(1, d) if l == DEPTH - 1 else None)
        return x, jnp.stack(states), ckv, krope

    grp_p = Group(rows=batch * seq, seq=seq, tm=1024, per_row=False)
    mods_p = [m[:pad].reshape(pad, 1, -1) for m in mods_all]
    kvm_p = kvm_all[:pad].reshape(pad, 1, -1)

    def mix_p(l, qs, k, v, og, lg):
        y, s = _gla_prompt(qs, k, v, og, lg, g_out_rows[l], batch, seq)
        return y.reshape(batch * seq, d), s

    def attend_p(qt, ckv, krope, kcat, ckvt):
        o = _attend_prompt(qt, kcat, ckvt, batch, seq, ATTN_TQ)
        return o.reshape(batch * seq, MLA_HEADS * KV_RANK)

    y_p, st_p, ckv_p, kr_p = trunk(grp_p, x_prompt.reshape(batch * seq, d), mods_p, kvm_p,
                                   jnp.arange(seq), mix_p, attend_p, BF16)

    grp_s = Group(rows=nseq, seq=nseq, tm=nseq, per_row=True)
    mods_s = [m[pad:] for m in mods_all]
    kvm_s = kvm_all[pad:]

    def mix_s(l, qs, k, v, og, lg):
        y, s = _gla_step(qs, k, v, og, lg, hg_g_out[l].reshape(HG_HEADS, HG_DV), state_hgrn, l)
        return y.reshape(nseq, d), s

    def attend_s(qt, ckv, krope, kcat, ckvt):
        qf = qt.reshape(QK_LAT, MLA_HEADS, nseq).transpose(2, 1, 0).astype(F32)
        o = _attend_paged(qf, ckv, krope, cache_ckv, cache_krope_t, page_table)
        return o.reshape(nseq, MLA_HEADS * KV_RANK).astype(BF16)

    y_s, st_s, ckv_s, kr_s = trunk(grp_s, x_sample.reshape(nseq, d), mods_s, kvm_s,
                                   jnp.full((nseq,), PAST_LEN), mix_s, attend_s, F32)

    return (y_p.reshape(batch, seq, d), y_s.reshape(nseq, 1, d),
            st_p.astype(state_hgrn.dtype), st_s.astype(state_hgrn.dtype),
            ckv_p.reshape(batch, seq, KV_RANK), kr_p.reshape(batch, seq, QK_ROPE),
            ckv_s.reshape(nseq, 1, KV_RANK), kr_s.reshape(nseq, 1, QK_ROPE))
```

```python
import functools
from typing import NamedTuple

import jax
import jax.numpy as jnp
import numpy as np
from jax import lax
from jax.experimental import pallas as pl
from jax.experimental.pallas import tpu as pltpu

D_MODEL = 1024
DEPTH = 2
N_A_LAYERS = DEPTH // 2
PAST_LEN = 16384
PAGE_SIZE = 128
D_FF = 2816
HG_HEADS = 8
HG_DK = D_MODEL // HG_HEADS
HG_DV = D_MODEL // HG_HEADS
MLA_HEADS = 8
QK_NOPE = 128
QK_ROPE = 64
V_HEAD = 128
KV_RANK = 256
Q_RANK = 384
QK_LAT = KV_RANK + QK_ROPE
ROPE_THETA = 10000.0
EPS = 1e-6
SM_SCALE = (QK_NOPE + QK_ROPE) ** -0.5

BF16 = jnp.bfloat16
F32 = jnp.float32

V7X_VMEM_BYTES = 64 * 1024 * 1024
VMEM_LIMIT = V7X_VMEM_BYTES * 3 // 4

GLA_CHUNK = 64
GLA_SUB = 16
GLA_BLOCK = 256
GLA_SAFE_DECAY = 60.0

ATTN_TQ = 512
VT_ROWS = KV_RANK + 16

NT_DIMS = (((1,), (1,)), ((), ()))


class Group(NamedTuple):
    rows: int
    seq: int
    tm: int
    per_row: bool

    @property
    def tiles_per_seq(self):
        return self.seq // self.tm


def _params(*sem):
    return pltpu.CompilerParams(dimension_semantics=sem, vmem_limit_bytes=VMEM_LIMIT)


def _dot(a, b):
    return jnp.dot(a, b, preferred_element_type=F32)


def _dot_nt(a, b):
    return lax.dot_general(a, b, NT_DIMS, preferred_element_type=F32)


def _sigmoid(x):
    return 1.0 / (1.0 + jnp.exp(-x))


def _rms(x):
    return x * lax.rsqrt(jnp.mean(x * x, axis=-1, keepdims=True) + EPS)


def _mod_norm(x, g, scale, shift):
    return (_rms(x) * g) * (1.0 + scale) + shift


def _mod_spec(grp, mods, col):
    if grp.per_row:
        return pl.BlockSpec((grp.tm, D_MODEL), lambda *g: (g[0], col))
    tps = grp.tiles_per_seq
    return pl.BlockSpec((None, 1, D_MODEL), lambda *g: (g[0] // tps, 0, col))


def _pos_spec(grp, width):
    if grp.per_row:
        return pl.BlockSpec((grp.tm, width), lambda *g: (g[0], 0))
    tps = grp.tiles_per_seq
    return pl.BlockSpec((grp.tm, width), lambda *g: (g[0] % tps, 0))


def _row_spec(grp, width):
    return pl.BlockSpec((grp.tm, width), lambda *g: (g[0], 0))


def _full_spec(shape):
    nd = len(shape)
    return pl.BlockSpec(shape, lambda *g: (0,) * nd)


def _ada_body(c_ref, w_ref, b_ref, o_ref):
    c = c_ref[...]
    sc = (c * _sigmoid(c)).astype(BF16)
    o_ref[...] = _dot(sc, w_ref[...].astype(BF16)) + b_ref[...]


def _ada(c_all, w, b, layer, tn=1024):
    rows = c_all.shape[0]
    n = w.shape[-1]
    return pl.pallas_call(
        _ada_body,
        out_shape=jax.ShapeDtypeStruct((rows, n), F32),
        grid=(n // tn,),
        in_specs=[
            _full_spec((rows, D_MODEL)),
            pl.BlockSpec((None, D_MODEL, tn), lambda j: (layer, 0, j)),
            pl.BlockSpec((None, 1, tn), lambda j: (layer, 0, j)),
        ],
        out_specs=pl.BlockSpec((rows, tn), lambda j: (0, j)),
        compiler_params=_params("arbitrary"),
        name="ada_proj",
    )(c_all, w, b)


FFN_TF = 256


def _ffn_body(x_ref, shift_ref, scale_ref, gate_ref, g_ref, wgu_hbm, wd_hbm, *rest, layer, half, final_norm):
    if final_norm:
        gf_ref, o_ref, h_scr, acc_scr, wg_buf, wu_buf, wd_buf, sems = rest
    else:
        o_ref, h_scr, acc_scr, wg_buf, wu_buf, wd_buf, sems = rest
    tf = FFN_TF
    nf = D_FF // tf
    i = pl.program_id(0)
    ni = pl.num_programs(0)

    def copies(j, slot):
        return [
            pltpu.make_async_copy(wgu_hbm.at[layer, half, :, pl.ds(j * tf, tf)], wg_buf.at[slot], sems.at[0, slot]),
            pltpu.make_async_copy(wgu_hbm.at[layer, half, :, pl.ds((nf + j) * tf, tf)], wu_buf.at[slot],
                                  sems.at[1, slot]),
            pltpu.make_async_copy(wd_hbm.at[layer, half, pl.ds(j * tf, tf), :], wd_buf.at[slot], sems.at[2, slot]),
        ]

    def start(j, slot):
        for cp in copies(j, slot):
            cp.start()

    def wait(j, slot):
        for cp in copies(j, slot):
            cp.wait()

    @pl.when(i == 0)
    def _():
        start(0, 0)

    h_scr[...] = _mod_norm(x_ref[...], g_ref[...], scale_ref[...], shift_ref[...]).astype(BF16)
    for j in range(nf):
        slot = (i * nf + j) % 2
        if j + 1 < nf:
            start(j + 1, 1 - slot)
        else:
            @pl.when(i + 1 < ni)
            def _():
                start(0, 1 - slot)
        wait(j, slot)
        h = h_scr[...]
        a = _dot(h, wg_buf[slot].astype(BF16))
        b = _dot(h, wu_buf[slot].astype(BF16))
        act = ((a * _sigmoid(a)) * b).astype(BF16)
        part = _dot(act, wd_buf[slot].astype(BF16))
        if j == 0:
            acc_scr[...] = part
        else:
            acc_scr[...] += part

    y = x_ref[...] + (0.5 * gate_ref[...]) * acc_scr[...]
    if final_norm:
        y = _rms(y) * gf_ref[...]
    o_ref[...] = y


def _ffn(grp, x, mods, sub, g_norm_row, w_gu, w_down, layer, half, g_final=None):
    tf = FFN_TF
    final_norm = g_final is not None
    in_specs = [
        _row_spec(grp, D_MODEL),
        _mod_spec(grp, mods, sub * 3 + 0),
        _mod_spec(grp, mods, sub * 3 + 1),
        _mod_spec(grp, mods, sub * 3 + 2),
        _full_spec((1, D_MODEL)),
        pl.BlockSpec(memory_space=pl.ANY),
        pl.BlockSpec(memory_space=pl.ANY),
    ]
    args = [x, mods, mods, mods, g_norm_row, w_gu, w_down]
    if final_norm:
        in_specs.append(_full_spec((1, D_MODEL)))
        args.append(g_final)
    return pl.pallas_call(
        functools.partial(_ffn_body, layer=layer, half=half, final_norm=final_norm),
        out_shape=jax.ShapeDtypeStruct((grp.rows, D_MODEL), F32),
        grid=(grp.rows // grp.tm,),
        in_specs=in_specs,
        out_specs=_row_spec(grp, D_MODEL),
        scratch_shapes=[pltpu.VMEM((grp.tm, D_MODEL), BF16), pltpu.VMEM((grp.tm, D_MODEL), F32),
                        pltpu.VMEM((2, D_MODEL, tf), F32), pltpu.VMEM((2, D_MODEL, tf), F32),
                        pltpu.VMEM((2, tf, D_MODEL), F32), pltpu.SemaphoreType.DMA((3, 2))],
        compiler_params=_params("arbitrary"),
        name="ffn_half",
    )(*args)


def _hgrn_in_body(x_ref, shift_ref, scale_ref, g_ref, w_ref, lb_ref, qs_ref, k_ref, v_ref, og_ref, lg_ref,
                  *, layer):
    d = D_MODEL
    h = _mod_norm(x_ref[...], g_ref[...], scale_ref[...], shift_ref[...]).astype(BF16)
    lbp = lb_ref[...]
    e = jnp.exp(lbp - jnp.max(lbp, axis=0, keepdims=True))
    lb = jnp.sum(e[: layer + 1], axis=0, keepdims=True) / jnp.sum(e, axis=0, keepdims=True)

    f = _dot(h, w_ref[:, d:2 * d])
    log_sig = jnp.minimum(f, 0.0) - jnp.log(1.0 + jnp.exp(-jnp.abs(f)))
    a = jnp.log(lb)
    b = jnp.log(1.0 - lb) + log_sig
    lg_ref[...] = jnp.maximum(a, b) + jnp.log(1.0 + jnp.exp(-jnp.abs(a - b)))
    k_ref[...] = ((1.0 - lb) / (1.0 + jnp.exp(f))).astype(k_ref.dtype)
    q = _dot(h, w_ref[:, 0:d])
    qs_ref[...] = (q * _sigmoid(q)).astype(qs_ref.dtype)
    g = _dot(h, w_ref[:, 3 * d:4 * d])
    og_ref[...] = (g * _sigmoid(g)).astype(og_ref.dtype)
    v_ref[...] = _dot(h, w_ref[:, 2 * d:3 * d]).astype(v_ref.dtype)


def _hgrn_in(grp, x, mods, g_norm_row, w_in, hg_lb, layer, stream_dtype):
    sub = 1
    outs = ([jax.ShapeDtypeStruct((grp.rows, D_MODEL), stream_dtype)] * 4
            + [jax.ShapeDtypeStruct((grp.rows, D_MODEL), F32)])
    return pl.pallas_call(
        functools.partial(_hgrn_in_body, layer=layer),
        out_shape=outs,
        grid=(grp.rows // grp.tm,),
        in_specs=[
            _row_spec(grp, D_MODEL),
            _mod_spec(grp, mods, sub * 3 + 0),
            _mod_spec(grp, mods, sub * 3 + 1),
            _full_spec((1, D_MODEL)),
            pl.BlockSpec((None, D_MODEL, 4 * D_MODEL), lambda i: (layer, 0, 0)),
            _full_spec(hg_lb.shape),
        ],
        out_specs=[_row_spec(grp, D_MODEL)] * 5,
        compiler_params=_params("parallel"),
        name="hgrn_in",
    )(x, mods, mods, g_norm_row, w_in, hg_lb)


def _segmented_cumsum(x, seg):
    row = lax.broadcasted_iota(jnp.int32, x.shape, 0) & (seg - 1)
    s = 1
    while s < seg:
        x = x + jnp.where(row >= s, pltpu.roll(x, s, 0), 0.0)
        s *= 2
    return x


def _gla_body(q_ref, k_ref, v_ref, og_ref, lg_ref, go_ref, y_ref, s_out_ref, st_scr, b_scr):
    c, sub = GLA_CHUNK, GLA_SUB
    t = pl.program_id(1)

    @pl.when(t == 0)
    def _():
        st_scr[...] = jnp.zeros_like(st_scr)

    b_scr[...] = _segmented_cumsum(lg_ref[...], c)
    tri = lax.broadcasted_iota(jnp.int32, (c, c), 1) <= lax.broadcasted_iota(jnp.int32, (c, c), 0)

    def finish(r0, h, o):
        cols = slice(h * HG_DV, (h + 1) * HG_DV)
        y = _rms(o) * go_ref[:, cols] * og_ref[pl.ds(r0, c), cols].astype(F32)
        y_ref[pl.ds(r0, c), cols] = y.astype(y_ref.dtype)

    n_chunks = q_ref.shape[0] // c

    def chunk_refs(r0):
        b_all = b_scr[pl.ds(r0, c), :]
        return b_all, [jnp.zeros((1, D_MODEL), F32)] + [b_all[i * sub - 1:i * sub, :] for i in range(1, c // sub)]

    worst = jnp.zeros((1, D_MODEL), F32)
    for ci in range(n_chunks):
        b_all, ref_rows = chunk_refs(ci * c)
        for i in range(c // sub):
            worst = jnp.maximum(worst, ref_rows[i] - b_all[(i + 1) * sub - 1:(i + 1) * sub, :])
    safe = jnp.max(worst) <= GLA_SAFE_DECAY

    @pl.when(safe)
    def _():
        for ci in range(n_chunks):
            r0 = ci * c
            b_all, ref_rows = chunk_refs(r0)
            stage = []
            for h in range(HG_HEADS):
                cols = slice(h * HG_DK, (h + 1) * HG_DK)
                b = b_all[:, cols]
                b_last = b[c - 1:c, :]
                refs = [r[:, cols] for r in ref_rows]
                q = q_ref[pl.ds(r0, c), cols].astype(F32)
                k = k_ref[pl.ds(r0, c), cols].astype(F32)
                v = v_ref[pl.ds(r0, c), cols]
                st = st_scr[h]
                o_inter = _dot_nt((q * jnp.exp(b)).astype(BF16), st.astype(BF16))
                rows = []
                for i in range(c // sub):
                    qi = (q[i * sub:(i + 1) * sub] * jnp.exp(b[i * sub:(i + 1) * sub] - refs[i])).astype(BF16)
                    ki = (k * jnp.exp(jnp.minimum(refs[i] - b, GLA_SAFE_DECAY))).astype(BF16)
                    rows.append(_dot_nt(qi, ki))
                kd = (k * jnp.exp(b_last - b)).astype(BF16)
                vt = v.astype(F32).T.astype(BF16)
                st_scr[h] = st * jnp.exp(b_last) + _dot(vt, kd)
                stage.append((o_inter, rows, v))
            for h, (o_inter, rows, v) in enumerate(stage):
                att = jnp.where(tri, jnp.concatenate(rows, axis=0), 0.0)
                finish(r0, h, o_inter + _dot(att.astype(BF16), v))

    @pl.when(jnp.logical_not(safe))
    def _():
        lane = lax.broadcasted_iota(jnp.int32, (HG_DV, c), 1)
        sublane = lax.broadcasted_iota(jnp.int32, (c, HG_DK), 0)

        def slow_chunk(ci, carry):
            r0 = pl.multiple_of(ci * c, c)
            for h in range(HG_HEADS):
                cols = slice(h * HG_DK, (h + 1) * HG_DK)
                vt = v_ref[pl.ds(r0, c), cols].astype(F32).T
                qf = q_ref[pl.ds(r0, c), cols].astype(F32)
                kf = k_ref[pl.ds(r0, c), cols].astype(F32)
                df = jnp.exp(lg_ref[pl.ds(r0, c), cols])

                def token(ti, carry2, vt=vt, qf=qf, kf=kf, df=df):
                    st, ot = carry2
                    sel = lane == ti
                    pick = lambda a: jnp.sum(jnp.where(sublane == ti, a, 0.0), axis=0, keepdims=True)
                    v_col = jnp.sum(jnp.where(sel, vt, 0.0), axis=1, keepdims=True)
                    st = st * pick(df) + v_col * pick(kf)
                    o_col = jnp.sum(st * pick(qf), axis=1, keepdims=True)
                    return st, jnp.where(sel, o_col, ot)

                st, ot = lax.fori_loop(0, c, token, (st_scr[h], jnp.zeros((HG_DV, c), F32)))
                st_scr[h] = st
                finish(r0, h, ot.T)
            return carry

        lax.fori_loop(0, n_chunks, slow_chunk, 0)

    @pl.when(t == pl.num_programs(1) - 1)
    def _():
        for h in range(HG_HEADS):
            s_out_ref[h] = st_scr[h].T


def _gla_prompt(qs, k, v, og, lg, g_out_row, batch, seq):
    tb = GLA_BLOCK
    stream = pl.BlockSpec((None, tb, D_MODEL), lambda b, t: (b, t, 0))
    shp = (batch, seq, D_MODEL)
    return pl.pallas_call(
        _gla_body,
        out_shape=[jax.ShapeDtypeStruct(shp, BF16),
                   jax.ShapeDtypeStruct((batch, HG_HEADS, HG_DK, HG_DV), F32)],
        grid=(batch, seq // tb),
        in_specs=[stream, stream, stream, stream, stream, _full_spec((1, D_MODEL))],
        out_specs=[stream, pl.BlockSpec((None, HG_HEADS, HG_DK, HG_DV), lambda b, t: (b, 0, 0, 0))],
        scratch_shapes=[pltpu.VMEM((HG_HEADS, HG_DV, HG_DK), F32), pltpu.VMEM((tb, D_MODEL), F32)],
        compiler_params=_params("parallel", "arbitrary"),
        name="gla_chunked",
    )(qs.reshape(shp), k.reshape(shp), v.reshape(shp), og.reshape(shp), lg.reshape(shp), g_out_row)


def _gla_step_body(q_ref, k_ref, v_ref, og_ref, lg_ref, go_ref, s_ref, y_ref, s_out_ref):
    go = go_ref[...]

    def one(bi, carry):
        qt = q_ref[bi].astype(F32).T
        kt = k_ref[bi].astype(F32).T
        dt = jnp.exp(lg_ref[bi]).T
        v = v_ref[bi].astype(F32)
        outs = []
        for h in range(HG_HEADS):
            s = s_ref[bi, h] * dt[:, h:h + 1] + kt[:, h:h + 1] * v[h:h + 1, :]
            s_out_ref[bi, h] = s
            outs.append(jnp.sum(s * qt[:, h:h + 1], axis=0, keepdims=True))
        o = jnp.concatenate(outs, axis=0)
        y_ref[bi] = (_rms(o) * go * og_ref[bi].astype(F32)).astype(y_ref.dtype)
        return carry

    lax.fori_loop(0, q_ref.shape[0], one, 0)


def _gla_step(qs, k, v, og, lg, g_out_heads, state, layer, bb=8):
    rows = qs.shape[0]
    shp = (rows, HG_HEADS, HG_DK)
    stream = pl.BlockSpec((bb, HG_HEADS, HG_DK), lambda i: (i, 0, 0))
    return pl.pallas_call(
        _gla_step_body,
        out_shape=[jax.ShapeDtypeStruct(shp, F32),
                   jax.ShapeDtypeStruct((rows, HG_HEADS, HG_DK, HG_DV), F32)],
        grid=(rows // bb,),
        in_specs=[stream, stream, stream, stream, stream, _full_spec((HG_HEADS, HG_DV)),
                  pl.BlockSpec((None, bb, HG_HEADS, HG_DK, HG_DV), lambda i: (layer, i, 0, 0, 0))],
        out_specs=[stream, pl.BlockSpec((bb, HG_HEADS, HG_DK, HG_DV), lambda i: (i, 0, 0, 0))],
        compiler_params=_params("parallel"),
        name="gla_step",
    )(qs.reshape(shp), k.reshape(shp), v.reshape(shp), og.reshape(shp), lg.reshape(shp), g_out_heads, state)


def _proj_res_body(a_ref, w_ref, x_ref, gate_ref, o_ref):
    o_ref[...] = x_ref[...] + gate_ref[...] * _dot(a_ref[...].astype(BF16), w_ref[...])


def _proj_res(grp, a, w, layer, x, mods, sub):
    kdim = a.shape[1]
    return pl.pallas_call(
        _proj_res_body,
        out_shape=jax.ShapeDtypeStruct((grp.rows, D_MODEL), F32),
        grid=(grp.rows // grp.tm,),
        in_specs=[_row_spec(grp, kdim),
                  pl.BlockSpec((None, kdim, D_MODEL), lambda i: (layer, 0, 0)),
                  _row_spec(grp, D_MODEL),
                  _mod_spec(grp, mods, sub * 3 + 2)],
        out_specs=_row_spec(grp, D_MODEL),
        compiler_params=_params("parallel"),
        name="proj_residual",
    )(a, w, x, mods)


def _kv_body(x_ref, shift_ref, scale_ref, g_ref, wc_ref, wr_ref, wrr_ref, gl_ref, cos_ref, sin_ref,
             ckv_ref, kr_ref, kcat_ref, ckvt_ref):
    h = _mod_norm(x_ref[...], g_ref[...], scale_ref[...], shift_ref[...]).astype(BF16)
    ckv = _rms(_dot(h, wc_ref[...])) * gl_ref[...]
    kr = _dot(h, wr_ref[...]) * cos_ref[...] + _dot(h, wrr_ref[...]) * sin_ref[...]
    ckv_ref[...] = ckv
    kr_ref[...] = kr
    kcat_ref[:, 0:KV_RANK] = ckv.astype(BF16)
    kcat_ref[:, KV_RANK:QK_LAT] = kr.astype(BF16)
    ckvt_ref[0:KV_RANK, :] = ckv.T.astype(BF16)
    ckvt_ref[KV_RANK:VT_ROWS, :] = jnp.ones((VT_ROWS - KV_RANK, ckvt_ref.shape[1]), BF16)


def _shared_kv(grp, x, kvmods, g_norm_row, w_c, w_r, w_rr, g_latent_row, cos_t, sin_t):
    tps = grp.tiles_per_seq
    return pl.pallas_call(
        _kv_body,
        out_shape=[jax.ShapeDtypeStruct((grp.rows, KV_RANK), F32),
                   jax.ShapeDtypeStruct((grp.rows, QK_ROPE), F32),
                   jax.ShapeDtypeStruct((grp.rows, QK_LAT), BF16),
                   jax.ShapeDtypeStruct((grp.rows // grp.seq, VT_ROWS, grp.seq), BF16)],
        grid=(grp.rows // grp.tm,),
        in_specs=[_row_spec(grp, D_MODEL),
                  _mod_spec(grp, kvmods, 0),
                  _mod_spec(grp, kvmods, 1),
                  _full_spec((1, D_MODEL)),
                  _full_spec(w_c.shape), _full_spec(w_r.shape), _full_spec(w_rr.shape),
                  _full_spec((1, KV_RANK)),
                  _pos_spec(grp, QK_ROPE), _pos_spec(grp, QK_ROPE)],
        out_specs=[_row_spec(grp, KV_RANK), _row_spec(grp, QK_ROPE), _row_spec(grp, QK_LAT),
                   pl.BlockSpec((None, VT_ROWS, grp.tm), lambda i: (i // tps, 0, i % tps))],
        compiler_params=_params("parallel"),
        name="shared_kv",
    )(x, kvmods, kvmods, g_norm_row, w_c, w_r, w_rr, g_latent_row, cos_t, sin_t)


def _mla_q_body(x_ref, shift_ref, scale_ref, g_ref, wdq_ref, gq_ref, wn_ref, wrt_ref, wrrt_ref, wukt_ref,
                cos_ref, sin_ref, q_ref):
    tm = x_ref.shape[0]
    h = _mod_norm(x_ref[...], g_ref[...], scale_ref[...], shift_ref[...]).astype(BF16)
    qc = (_rms(_dot(h, wdq_ref[...])) * gq_ref[...]).astype(BF16)
    qn = _dot(qc, wn_ref[...])
    qrt = (_dot_nt(wrt_ref[...], qc) * cos_ref[...] + _dot_nt(wrrt_ref[...], qc) * sin_ref[...]) * SM_SCALE
    for hd in range(MLA_HEADS):
        cols = slice(hd * tm, (hd + 1) * tm)
        qlt = _dot_nt(wukt_ref[hd], qn[:, hd * QK_NOPE:(hd + 1) * QK_NOPE].astype(BF16)) * SM_SCALE
        q_ref[0:KV_RANK, cols] = qlt.astype(q_ref.dtype)
        q_ref[KV_RANK:QK_LAT, cols] = qrt[hd * QK_ROPE:(hd + 1) * QK_ROPE, :].astype(q_ref.dtype)


def _mla_q(grp, x, mods, g_norm_row, w_dq, g_q_row, w_n, w_rt, w_rrt, w_ukt, cos_t, sin_t, layer_b):
    sub = 1
    nseq = grp.rows // grp.seq
    tps = grp.tiles_per_seq
    width = MLA_HEADS * QK_ROPE
    sel = lambda a: pl.BlockSpec((None,) + a.shape[1:], lambda i: (layer_b,) + (0,) * (a.ndim - 1))
    if grp.per_row:
        pos = pl.BlockSpec((width, grp.tm), lambda i: (0, i))
    else:
        pos = pl.BlockSpec((width, grp.tm), lambda i: (0, i % tps))
    return pl.pallas_call(
        _mla_q_body,
        out_shape=jax.ShapeDtypeStruct((nseq, tps, QK_LAT, MLA_HEADS * grp.tm), BF16),
        grid=(grp.rows // grp.tm,),
        in_specs=[_row_spec(grp, D_MODEL),
                  _mod_spec(grp, mods, sub * 3 + 0),
                  _mod_spec(grp, mods, sub * 3 + 1),
                  _full_spec((1, D_MODEL)),
                  sel(w_dq), sel(g_q_row), sel(w_n), sel(w_rt), sel(w_rrt),
                  _full_spec(w_ukt.shape),
                  pos, pos],
        out_specs=pl.BlockSpec((None, None, QK_LAT, MLA_HEADS * grp.tm), lambda i: (i // tps, i % tps, 0, 0)),
        compiler_params=_params("parallel"),
        name="mla_q",
    )(x, mods, mods, g_norm_row, w_dq, g_q_row, w_n, w_rt, w_rrt, w_ukt, cos_t, sin_t)


def _attn_body(qi_ref, kj_ref, qt_ref, k_ref, vt_ref, o_ref, m_scr, acc_scr, *, tq):
    p = pl.program_id(1)
    i = qi_ref[p]
    j = kj_ref[p]

    @pl.when(j == 0)
    def _():
        m_scr[...] = jnp.full_like(m_scr, -jnp.inf)
        acc_scr[...] = jnp.zeros_like(acc_scr)

    def update(masked):
        s = _dot(k_ref[...], qt_ref[...])
        if masked:
            kpos = lax.broadcasted_iota(jnp.int32, s.shape, 0)
            qpos = lax.broadcasted_iota(jnp.int32, s.shape, 1) & (tq - 1)
            s = jnp.where(kpos <= qpos, s, -jnp.inf)
        m_prev = m_scr[...]
        m_new = jnp.maximum(m_prev, jnp.max(s, axis=0, keepdims=True))
        alpha = jnp.exp(m_prev - m_new)
        pr = jnp.exp(s - m_new).astype(BF16)
        acc_scr[...] = alpha * acc_scr[...] + _dot(vt_ref[...], pr)
        m_scr[...] = m_new

    @pl.when(j < i)
    def _():
        update(False)

    @pl.when(j == i)
    def _():
        update(True)
        out = acc_scr[0:KV_RANK, :] / acc_scr[KV_RANK:KV_RANK + 1, :]
        for hd in range(MLA_HEADS):
            o_ref[:, hd * KV_RANK:(hd + 1) * KV_RANK] = out[:, hd * tq:(hd + 1) * tq].T.astype(o_ref.dtype)


def _attend_prompt(qt, kcat, ckvt, batch, seq, tq):
    nq = seq // tq
    pairs = [(i, j) for i in range(nq) for j in range(i + 1)]
    qi = jnp.asarray(np.array([p[0] for p in pairs], np.int32))
    kj = jnp.asarray(np.array([p[1] for p in pairs], np.int32))
    cols = MLA_HEADS * tq
    return pl.pallas_call(
        functools.partial(_attn_body, tq=tq),
        out_shape=jax.ShapeDtypeStruct((batch, seq, MLA_HEADS * KV_RANK), BF16),
        grid_spec=pltpu.PrefetchScalarGridSpec(
            num_scalar_prefetch=2,
            grid=(batch, len(pairs)),
            in_specs=[pl.BlockSpec((None, None, QK_LAT, cols), lambda b, p, qi, kj: (b, qi[p], 0, 0)),
                      pl.BlockSpec((None, tq, QK_LAT), lambda b, p, qi, kj: (b, kj[p], 0)),
                      pl.BlockSpec((None, VT_ROWS, tq), lambda b, p, qi, kj: (b, 0, kj[p]))],
            out_specs=pl.BlockSpec((None, tq, MLA_HEADS * KV_RANK), lambda b, p, qi, kj: (b, qi[p], 0)),
            scratch_shapes=[pltpu.VMEM((1, cols), F32), pltpu.VMEM((VT_ROWS, cols), F32)],
        ),
        compiler_params=_params("parallel", "arbitrary"),
        name="attend_prompt",
    )(qi, kj, qt, kcat.reshape(batch, seq, QK_LAT), ckvt)


PAGED_CHUNK_PAGES = 32
PAGED_SLOTS = 4


def _paged_body(pt_ref, q_ref, cn_ref, kn_ref, ckv_hbm, krt_hbm, o_ref, cbuf, kbuf, sems, *, n_pages):
    ch = PAGED_CHUNK_PAGES
    nc = n_pages // ch
    ahead = PAGED_SLOTS - 1
    nseq = pl.num_programs(0)
    b = pl.program_id(0)

    def copies(g, slot):
        out = []
        for pg in range(ch):
            page = pt_ref[g * ch + pg]
            dst = pl.ds(pg * PAGE_SIZE, PAGE_SIZE)
            out.append(pltpu.make_async_copy(ckv_hbm.at[page], cbuf.at[slot, dst, :], sems.at[0, slot]))
            out.append(pltpu.make_async_copy(krt_hbm.at[page], kbuf.at[slot, pg], sems.at[1, slot]))
        return out

    def start(g, slot):
        for cp in copies(g, slot):
            cp.start()

    def wait(slot):
        for cp in copies(0, slot):
            cp.wait()

    @pl.when(b == 0)
    def _():
        for c in range(ahead):
            start(c, c)

    q = q_ref[...]
    ql = q[:, 0:KV_RANK]
    qr = q[:, KV_RANK:QK_LAT]
    cn = cn_ref[...]
    kn = kn_ref[...]
    m = jnp.sum(ql * cn, axis=1, keepdims=True) + jnp.sum(qr * kn, axis=1, keepdims=True)
    l = jnp.ones_like(m)
    acc = jnp.broadcast_to(cn, (MLA_HEADS, KV_RANK))
    qlb = ql.astype(BF16)
    qrb = qr.astype(BF16)
    for c in range(nc):
        slot = c % PAGED_SLOTS
        g = b * nc + c
        nxt = (c + ahead) % PAGED_SLOTS
        if c + ahead < nc:
            start(g + ahead, nxt)
        else:
            @pl.when(b + 1 < nseq)
            def _():
                start(g + ahead, nxt)
        wait(slot)
        kc = cbuf[slot].astype(BF16)
        s_rope = [_dot(qrb, kbuf[slot, pg].astype(BF16)) for pg in range(ch)]
        s = _dot_nt(qlb, kc) + jnp.concatenate(s_rope, axis=1)
        m_new = jnp.maximum(m, jnp.max(s, axis=1, keepdims=True))
        alpha = jnp.exp(m - m_new)
        pr = jnp.exp(s - m_new)
        l = alpha * l + jnp.sum(pr, axis=1, keepdims=True)
        acc = alpha * acc + _dot(pr.astype(BF16), kc)
        m = m_new
    o_ref[...] = acc / l


def _attend_paged(q, ckv_new, krope_new, cache_ckv, cache_krope_t, page_table):
    nseq, n_pages = page_table.shape
    ch = PAGED_CHUNK_PAGES
    assert n_pages % (PAGED_SLOTS * ch) == 0
    per_seq = lambda *tail: pl.BlockSpec((None,) + tail, lambda i, pt: (i,) + (0,) * len(tail))
    return pl.pallas_call(
        functools.partial(_paged_body, n_pages=n_pages),
        out_shape=jax.ShapeDtypeStruct((nseq, MLA_HEADS, KV_RANK), F32),
        grid_spec=pltpu.PrefetchScalarGridSpec(
            num_scalar_prefetch=1,
            grid=(nseq,),
            in_specs=[per_seq(MLA_HEADS, QK_LAT), per_seq(1, KV_RANK), per_seq(1, QK_ROPE),
                      pl.BlockSpec(memory_space=pl.ANY), pl.BlockSpec(memory_space=pl.ANY)],
            out_specs=per_seq(MLA_HEADS, KV_RANK),
            scratch_shapes=[pltpu.VMEM((PAGED_SLOTS, ch * PAGE_SIZE, KV_RANK), F32),
                            pltpu.VMEM((PAGED_SLOTS, ch, QK_ROPE, PAGE_SIZE), F32),
                            pltpu.SemaphoreType.DMA((2, PAGED_SLOTS))],
        ),
        compiler_params=_params("arbitrary"),
        name="attend_paged",
    )(page_table.reshape(-1), q, ckv_new.reshape(nseq, 1, KV_RANK), krope_new.reshape(nseq, 1, QK_ROPE),
      cache_ckv, cache_krope_t)


def _attn_out_body(o_ref, wuv_ref, wo_ref, x_ref, gate_ref, out_ref):
    parts = [_dot(o_ref[:, hd * KV_RANK:(hd + 1) * KV_RANK], wuv_ref[hd]).astype(BF16)
             for hd in range(MLA_HEADS)]
    o = jnp.concatenate(parts, axis=1)
    out_ref[...] = x_ref[...] + gate_ref[...] * _dot(o, wo_ref[...])


def _attn_out(grp, o_lat, w_uv_t, w_out, layer_b, x, mods, sub):
    width = MLA_HEADS * KV_RANK
    return pl.pallas_call(
        _attn_out_body,
        out_shape=jax.ShapeDtypeStruct((grp.rows, D_MODEL), F32),
        grid=(grp.rows // grp.tm,),
        in_specs=[_row_spec(grp, width),
                  _full_spec(w_uv_t.shape),
                  pl.BlockSpec((None, MLA_HEADS * V_HEAD, D_MODEL), lambda i: (layer_b, 0, 0)),
                  _row_spec(grp, D_MODEL),
                  _mod_spec(grp, mods, sub * 3 + 2)],
        out_specs=_row_spec(grp, D_MODEL),
        compiler_params=_params("parallel"),
        name="attn_out",
    )(o_lat, w_uv_t, w_out, x, mods)


def _rope_tables(pos):
    half = QK_ROPE // 2
    inv_freq = ROPE_THETA ** (-2.0 * jnp.arange(half, dtype=F32) / QK_ROPE)
    ang = pos.astype(F32)[:, None] * inv_freq[None, :]
    cos, sin = jnp.cos(ang), jnp.sin(ang)
    return jnp.concatenate([cos, cos], axis=-1), jnp.concatenate([-sin, sin], axis=-1)


def _swap_halves(w):
    half = QK_ROPE // 2
    return jnp.concatenate([w[..., half:], w[..., :half]], axis=-1)


def kernel(x_prompt, x_sample, c_prompt, c_sample, state_hgrn, cache_ckv, cache_krope, page_table, w_ada, b_ada, g_norm, w_ffn_gu, w_ffn_down, hg_w_in, hg_lb, hg_g_out, hg_w_out, kv_w_ada, kv_b_ada, kv_g_norm, kv_w_down, kv_g_latent, kv_w_uk, kv_w_uv, q_w_down, q_g_norm, q_w_up, attn_w_out, g_final):
    batch, seq, d = x_prompt.shape
    nseq = x_sample.shape[0]
    n_b = DEPTH - N_A_LAYERS

    w_in = hg_w_in.astype(BF16)
    w_ho = hg_w_out.astype(BF16)
    w_ao = attn_w_out.astype(BF16)
    w_kc = kv_w_down[:, :KV_RANK].astype(BF16)
    w_kr = kv_w_down[:, KV_RANK:].astype(BF16)
    w_krr = _swap_halves(kv_w_down[:, KV_RANK:]).astype(BF16)
    w_dq = q_w_down.astype(BF16)
    w_qn = q_w_up[..., :QK_NOPE].reshape(n_b, Q_RANK, MLA_HEADS * QK_NOPE).astype(BF16)
    w_qr = q_w_up[..., QK_NOPE:].reshape(n_b, Q_RANK, MLA_HEADS * QK_ROPE)
    w_qrr = _swap_halves(q_w_up[..., QK_NOPE:]).reshape(n_b, Q_RANK, MLA_HEADS * QK_ROPE)
    w_qrt = w_qr.transpose(0, 2, 1).astype(BF16)
    w_qrrt = w_qrr.transpose(0, 2, 1).astype(BF16)
    w_ukt = kv_w_uk.transpose(1, 0, 2).astype(BF16)
    w_uv_t = kv_w_uv.transpose(1, 0, 2).astype(BF16)
    cache_krope_t = cache_krope.transpose(0, 2, 1)
    g_q_rows = q_g_norm.reshape(n_b, 1, Q_RANK)
    g_out_rows = hg_g_out.reshape(N_A_LAYERS, 1, D_MODEL)

    pad = 8
    c_all = jnp.concatenate([c_prompt, jnp.zeros((pad - batch, d), F32), c_sample], axis=0)
    mods_all = [_ada(c_all, w_ada, b_ada.reshape(DEPTH, 1, -1), l) for l in range(DEPTH)]
    kvm_all = _ada(c_all, kv_w_ada[None], kv_b_ada.reshape(1, 1, -1), 0)

    def trunk(grp, x, mods_l, kvmods, pos, mix, attend, stream_dtype):
        grp_half = grp._replace(tm=max(grp.tm // 2, 128))
        grp_q = grp._replace(tm=min(grp.tm, ATTN_TQ))
        cos_k, sin_k = _rope_tables(pos)
        cos_qt, sin_qt = jnp.tile(cos_k, (1, MLA_HEADS)).T, jnp.tile(sin_k, (1, MLA_HEADS)).T
        states = []
        ckv = krope = kcat = ckvt = None
        for l in range(DEPTH):
            mods = mods_l[l]
            gn = lambda j: g_norm[l, j].reshape(1, d)
            if l == N_A_LAYERS:
                ckv, krope, kcat, ckvt = _shared_kv(grp, x, kvmods, kv_g_norm.reshape(1, d), w_kc, w_kr, w_krr,
                                                    kv_g_latent.reshape(1, KV_RANK), cos_k, sin_k)
            x = _ffn(grp, x, mods, 0, gn(0), w_ffn_gu, w_ffn_down, l, 0)
            if l < N_A_LAYERS:
                qs, k, v, og, lg = _hgrn_in(grp_half, x, mods, gn(1), w_in, hg_lb, l, stream_dtype)
                y, s = mix(l, qs, k, v, og, lg)
                states.append(s)
                x = _proj_res(grp, y, w_ho, l, x, mods, 1)
            else:
                lb_ = l - N_A_LAYERS
                qt = _mla_q(grp_q, x, mods, gn(1), w_dq, g_q_rows, w_qn, w_qrt, w_qrrt, w_ukt, cos_qt, sin_qt,
                            lb_)
                o_lat = attend(qt, ckv, krope, kcat, ckvt)
                x = _attn_out(grp, o_lat, w_uv_t, w_ao, lb_, x, mods, 1)
            x = _ffn(grp, x, mods, 2, gn(2), w_ffn_gu, w_ffn_down, l, 1,
                     g_final=g_final.reshape(1, d) if l == DEPTH - 1 else None)
        return x, jnp.stack(states), ckv, krope

    grp_p = Group(rows=batch * seq, seq=seq, tm=1024, per_row=False)
    mods_p = [m[:pad].reshape(pad, 1, -1) for m in mods_all]
    kvm_p = kvm_all[:pad].reshape(pad, 1, -1)

    def mix_p(l, qs, k, v, og, lg):
        y, s = _gla_prompt(qs, k, v, og, lg, g_out_rows[l], batch, seq)
        return y.reshape(batch * seq, d), s

    def attend_p(qt, ckv, krope, kcat, ckvt):
        o = _attend_prompt(qt, kcat, ckvt, batch, seq, ATTN_TQ)
        return o.reshape(batch * seq, MLA_HEADS * KV_RANK)

    y_p, st_p, ckv_p, kr_p = trunk(grp_p, x_prompt.reshape(batch * seq, d), mods_p, kvm_p,
                                   jnp.arange(seq), mix_p, attend_p, BF16)

    grp_s = Group(rows=nseq, seq=nseq, tm=nseq, per_row=True)
    mods_s = [m[pad:] for m in mods_all]
    kvm_s = kvm_all[pad:]

    def mix_s(l, qs, k, v, og, lg):
        y, s = _gla_step(qs, k, v, og, lg, hg_g_out[l].reshape(HG_HEADS, HG_DV), state_hgrn, l)
        return y.reshape(nseq, d), s

    def attend_s(qt, ckv, krope, kcat, ckvt):
        qf = qt.reshape(QK_LAT, MLA_HEADS, nseq).transpose(2, 1, 0).astype(F32)
        o = _attend_paged(qf, ckv, krope, cache_ckv, cache_krope_t, page_table)
        return o.reshape(nseq, MLA_HEADS * KV_RANK).astype(BF16)

    y_s, st_s, ckv_s, kr_s = trunk(grp_s, x_sample.reshape(nseq, d), mods_s, kvm_s,
                                   jnp.full((nseq,), PAST_LEN), mix_s, attend_s, F32)

    return (y_p.reshape(batch, seq, d), y_s.reshape(nseq, 1, d),
            st_p.astype(state_hgrn.dtype), st_s.astype(state_hgrn.dtype),
            ckv_p.reshape(batch, seq, KV_RANK), kr_p.reshape(batch, seq, QK_ROPE),
            ckv_s.reshape(nseq, 1, KV_RANK), kr_s.reshape(nseq, 1, QK_ROPE))
```

```python
import functools
from typing import NamedTuple

import jax
import jax.numpy as jnp
import numpy as np
from jax import lax
from jax.experimental import pallas as pl
from jax.experimental.pallas import tpu as pltpu

D_MODEL = 1024
DEPTH = 2
N_A_LAYERS = DEPTH // 2
PAST_LEN = 16384
PAGE_SIZE = 128
D_FF = 2816
HG_HEADS = 8
HG_DK = D_MODEL // HG_HEADS
HG_DV = D_MODEL // HG_HEADS
MLA_HEADS = 8
QK_NOPE = 128
QK_ROPE = 64
V_HEAD = 128
KV_RANK = 256
Q_RANK = 384
QK_LAT = KV_RANK + QK_ROPE
ROPE_THETA = 10000.0
EPS = 1e-6
SM_SCALE = (QK_NOPE + QK_ROPE) ** -0.5

BF16 = jnp.bfloat16
F32 = jnp.float32

V7X_VMEM_BYTES = 64 * 1024 * 1024
VMEM_LIMIT = V7X_VMEM_BYTES * 3 // 4

PROMPT_TM = 1024
WIDE_TM = 512
MIN_TM = 128
ADA_TN = 1024
STEP_SEQS = 8

GLA_CHUNK = 64
GLA_SUB = 16
GLA_BLOCK = 256
GLA_SAFE_DECAY = 60.0

ATTN_TQ = 512
VT_ROWS = KV_RANK + 16

NT_DIMS = (((1,), (1,)), ((), ()))


class Group(NamedTuple):
    rows: int
    seq: int
    tm: int
    per_row: bool

    @property
    def tiles_per_seq(self):
        return self.seq // self.tm


def _params(*sem):
    return pltpu.CompilerParams(dimension_semantics=sem, vmem_limit_bytes=VMEM_LIMIT)


def _dot(a, b):
    return jnp.dot(a, b, preferred_element_type=F32)


def _dot_nt(a, b):
    return lax.dot_general(a, b, NT_DIMS, preferred_element_type=F32)


def _sigmoid(x):
    return 1.0 / (1.0 + jnp.exp(-x))


def _rms(x):
    return x * lax.rsqrt(jnp.mean(x * x, axis=-1, keepdims=True) + EPS)


def _mod_norm(x, g, scale, shift):
    return (_rms(x) * g) * (1.0 + scale) + shift


def _mod_spec(grp, mods, col):
    if grp.per_row:
        return pl.BlockSpec((grp.tm, D_MODEL), lambda *g: (g[0], col))
    tps = grp.tiles_per_seq
    return pl.BlockSpec((None, 1, D_MODEL), lambda *g: (g[0] // tps, 0, col))


def _pos_spec(grp, width):
    if grp.per_row:
        return pl.BlockSpec((grp.tm, width), lambda *g: (g[0], 0))
    tps = grp.tiles_per_seq
    return pl.BlockSpec((grp.tm, width), lambda *g: (g[0] % tps, 0))


def _row_spec(grp, width):
    return pl.BlockSpec((grp.tm, width), lambda *g: (g[0], 0))


def _full_spec(shape):
    nd = len(shape)
    return pl.BlockSpec(shape, lambda *g: (0,) * nd)


def _ada_body(c_ref, w_ref, b_ref, o_ref):
    c = c_ref[...]
    sc = (c * _sigmoid(c)).astype(BF16)
    o_ref[...] = _dot(sc, w_ref[...].astype(BF16)) + b_ref[...]


def _ada(c_all, w, b, layer, tn=ADA_TN):
    rows = c_all.shape[0]
    n = w.shape[-1]
    return pl.pallas_call(
        _ada_body,
        out_shape=jax.ShapeDtypeStruct((rows, n), F32),
        grid=(n // tn,),
        in_specs=[
            _full_spec((rows, D_MODEL)),
            pl.BlockSpec((None, D_MODEL, tn), lambda j: (layer, 0, j)),
            pl.BlockSpec((None, 1, tn), lambda j: (layer, 0, j)),
        ],
        out_specs=pl.BlockSpec((rows, tn), lambda j: (0, j)),
        compiler_params=_params("arbitrary"),
        name="ada_proj",
    )(c_all, w, b)


FFN_TF = 256
FFN_GROUP = 3


def _ffn_body(x_ref, shift_ref, scale_ref, gate_ref, g_ref, wgu_hbm, wd_hbm, *rest, layer, half, final_norm):
    if final_norm:
        gf_ref, o_ref, h_scr, acc_scr, wg_buf, wu_buf, wd_buf, sems = rest
    else:
        o_ref, h_scr, acc_scr, wg_buf, wu_buf, wd_buf, sems = rest
    tf = FFN_TF
    nf = D_FF // tf
    groups = [list(range(s, min(s + FFN_GROUP, nf))) for s in range(0, nf, FFN_GROUP)]
    ng = len(groups)
    assert ng % 2 == 0
    i = pl.program_id(0)
    ni = pl.num_programs(0)

    def copies(g, slot):
        out = []
        for t, j in enumerate(groups[g]):
            out += [
                pltpu.make_async_copy(wgu_hbm.at[layer, half, :, pl.ds(j * tf, tf)], wg_buf.at[slot, t],
                                      sems.at[0, slot]),
                pltpu.make_async_copy(wgu_hbm.at[layer, half, :, pl.ds((nf + j) * tf, tf)], wu_buf.at[slot, t],
                                      sems.at[1, slot]),
                pltpu.make_async_copy(wd_hbm.at[layer, half, pl.ds(j * tf, tf), :], wd_buf.at[slot, t],
                                      sems.at[2, slot]),
            ]
        return out

    def start(g, slot):
        for cp in copies(g, slot):
            cp.start()

    def wait(g, slot):
        for cp in copies(g, slot):
            cp.wait()

    @pl.when(i == 0)
    def _():
        start(0, 0)

    h_scr[...] = _mod_norm(x_ref[...], g_ref[...], scale_ref[...], shift_ref[...]).astype(BF16)
    for g in range(ng):
        slot = g % 2
        if g + 1 < ng:
            start(g + 1, 1 - slot)
        else:
            @pl.when(i + 1 < ni)
            def _():
                start(0, 1 - slot)
        wait(g, slot)
        for t, j in enumerate(groups[g]):
            h = h_scr[...]
            a = _dot(h, wg_buf[slot, t].astype(BF16))
            b = _dot(h, wu_buf[slot, t].astype(BF16))
            act = ((a * _sigmoid(a)) * b).astype(BF16)
            part = _dot(act, wd_buf[slot, t].astype(BF16))
            if j == 0:
                acc_scr[...] = part
            else:
                acc_scr[...] += part

    y = x_ref[...] + (0.5 * gate_ref[...]) * acc_scr[...]
    if final_norm:
        y = _rms(y) * gf_ref[...]
    o_ref[...] = y


def _ffn(grp, x, mods, sub, g_norm_row, w_gu, w_down, layer, half, g_final=None):
    tf = FFN_TF
    final_norm = g_final is not None
    in_specs = [
        _row_spec(grp, D_MODEL),
        _mod_spec(grp, mods, sub * 3 + 0),
        _mod_spec(grp, mods, sub * 3 + 1),
        _mod_spec(grp, mods, sub * 3 + 2),
        _full_spec((1, D_MODEL)),
        pl.BlockSpec(memory_space=pl.ANY),
        pl.BlockSpec(memory_space=pl.ANY),
    ]
    args = [x, mods, mods, mods, g_norm_row, w_gu, w_down]
    if final_norm:
        in_specs.append(_full_spec((1, D_MODEL)))
        args.append(g_final)
    return pl.pallas_call(
        functools.partial(_ffn_body, layer=layer, half=half, final_norm=final_norm),
        out_shape=jax.ShapeDtypeStruct((grp.rows, D_MODEL), F32),
        grid=(grp.rows // grp.tm,),
        in_specs=in_specs,
        out_specs=_row_spec(grp, D_MODEL),
        scratch_shapes=[pltpu.VMEM((grp.tm, D_MODEL), BF16), pltpu.VMEM((grp.tm, D_MODEL), F32),
                        pltpu.VMEM((2, FFN_GROUP, D_MODEL, tf), F32), pltpu.VMEM((2, FFN_GROUP, D_MODEL, tf), F32),
                        pltpu.VMEM((2, FFN_GROUP, tf, D_MODEL), F32), pltpu.SemaphoreType.DMA((3, 2))],
        compiler_params=_params("arbitrary"),
        name="ffn_half",
    )(*args)


def _hgrn_in_body(x_ref, shift_ref, scale_ref, g_ref, w_ref, lb_ref, qs_ref, k_ref, v_ref, og_ref, lg_ref,
                  *, layer):
    d = D_MODEL
    h = _mod_norm(x_ref[...], g_ref[...], scale_ref[...], shift_ref[...]).astype(BF16)
    lbp = lb_ref[...]
    e = jnp.exp(lbp - jnp.max(lbp, axis=0, keepdims=True))
    lb = jnp.sum(e[: layer + 1], axis=0, keepdims=True) / jnp.sum(e, axis=0, keepdims=True)

    f = _dot(h, w_ref[:, d:2 * d])
    q = _dot(h, w_ref[:, 0:d])
    g = _dot(h, w_ref[:, 3 * d:4 * d])
    v_ref[...] = _dot(h, w_ref[:, 2 * d:3 * d]).astype(v_ref.dtype)
    log_sig = jnp.minimum(f, 0.0) - jnp.log(1.0 + jnp.exp(-jnp.abs(f)))
    a = jnp.log(lb)
    b = jnp.log(1.0 - lb) + log_sig
    lg_ref[...] = jnp.maximum(a, b) + jnp.log(1.0 + jnp.exp(-jnp.abs(a - b)))
    k_ref[...] = ((1.0 - lb) / (1.0 + jnp.exp(f))).astype(k_ref.dtype)
    qs_ref[...] = (q * _sigmoid(q)).astype(qs_ref.dtype)
    og_ref[...] = (g * _sigmoid(g)).astype(og_ref.dtype)


def _hgrn_in(grp, x, mods, g_norm_row, w_in, hg_lb, layer, stream_dtype):
    sub = 1
    outs = ([jax.ShapeDtypeStruct((grp.rows, D_MODEL), stream_dtype)] * 4
            + [jax.ShapeDtypeStruct((grp.rows, D_MODEL), F32)])
    return pl.pallas_call(
        functools.partial(_hgrn_in_body, layer=layer),
        out_shape=outs,
        grid=(grp.rows // grp.tm,),
        in_specs=[
            _row_spec(grp, D_MODEL),
            _mod_spec(grp, mods, sub * 3 + 0),
            _mod_spec(grp, mods, sub * 3 + 1),
            _full_spec((1, D_MODEL)),
            pl.BlockSpec((None, D_MODEL, 4 * D_MODEL), lambda i: (layer, 0, 0)),
            _full_spec(hg_lb.shape),
        ],
        out_specs=[_row_spec(grp, D_MODEL)] * 5,
        compiler_params=_params("parallel"),
        name="hgrn_in",
    )(x, mods, mods, g_norm_row, w_in, hg_lb)


def _segmented_cumsum(x, seg):
    row = lax.broadcasted_iota(jnp.int32, x.shape, 0) & (seg - 1)
    s = 1
    while s < seg:
        x = x + jnp.where(row >= s, pltpu.roll(x, s, 0), 0.0)
        s *= 2
    return x


def _gla_body(q_ref, k_ref, v_ref, og_ref, lg_ref, go_ref, y_ref, s_out_ref, st_scr, b_scr):
    c, sub = GLA_CHUNK, GLA_SUB
    t = pl.program_id(1)

    @pl.when(t == 0)
    def _():
        st_scr[...] = jnp.zeros_like(st_scr)

    b_scr[...] = _segmented_cumsum(lg_ref[...], c)
    tri = lax.broadcasted_iota(jnp.int32, (c, c), 1) <= lax.broadcasted_iota(jnp.int32, (c, c), 0)

    def finish(r0, h, o):
        cols = slice(h * HG_DV, (h + 1) * HG_DV)
        y = _rms(o) * go_ref[:, cols] * og_ref[pl.ds(r0, c), cols].astype(F32)
        y_ref[pl.ds(r0, c), cols] = y.astype(y_ref.dtype)

    n_chunks = q_ref.shape[0] // c

    def chunk_refs(r0):
        b_all = b_scr[pl.ds(r0, c), :]
        return b_all, [jnp.zeros((1, D_MODEL), F32)] + [b_all[i * sub - 1:i * sub, :] for i in range(1, c // sub)]

    worst = jnp.zeros((1, D_MODEL), F32)
    for ci in range(n_chunks):
        b_all, ref_rows = chunk_refs(ci * c)
        for i in range(c // sub):
            worst = jnp.maximum(worst, ref_rows[i] - b_all[(i + 1) * sub - 1:(i + 1) * sub, :])
    safe = jnp.max(worst) <= GLA_SAFE_DECAY

    @pl.when(safe)
    def _():
        for ci in range(n_chunks):
            r0 = ci * c
            b_all, ref_rows = chunk_refs(r0)
            stage = []
            for h in range(HG_HEADS):
                cols = slice(h * HG_DK, (h + 1) * HG_DK)
                b = b_all[:, cols]
                b_last = b[c - 1:c, :]
                refs = [r[:, cols] for r in ref_rows]
                q = q_ref[pl.ds(r0, c), cols].astype(F32)
                k = k_ref[pl.ds(r0, c), cols].astype(F32)
                v = v_ref[pl.ds(r0, c), cols]
                st = st_scr[h]
                o_inter = _dot_nt((q * jnp.exp(b)).astype(BF16), st.astype(BF16))
                rows = []
                for i in range(c // sub):
                    qi = (q[i * sub:(i + 1) * sub] * jnp.exp(b[i * sub:(i + 1) * sub] - refs[i])).astype(BF16)
                    ki = (k * jnp.exp(jnp.minimum(refs[i] - b, GLA_SAFE_DECAY))).astype(BF16)
                    rows.append(_dot_nt(qi, ki))
                kd = (k * jnp.exp(b_last - b)).astype(BF16)
                vt = v.astype(F32).T.astype(BF16)
                st_scr[h] = st * jnp.exp(b_last) + _dot(vt, kd)
                stage.append((o_inter, rows, v))
            for h, (o_inter, rows, v) in enumerate(stage):
                att = jnp.where(tri, jnp.concatenate(rows, axis=0), 0.0)
                finish(r0, h, o_inter + _dot(att.astype(BF16), v))

    @pl.when(jnp.logical_not(safe))
    def _():
        lane = lax.broadcasted_iota(jnp.int32, (HG_DV, c), 1)
        sublane = lax.broadcasted_iota(jnp.int32, (c, HG_DK), 0)

        def slow_chunk(ci, carry):
            r0 = pl.multiple_of(ci * c, c)
            for h in range(HG_HEADS):
                cols = slice(h * HG_DK, (h + 1) * HG_DK)
                vt = v_ref[pl.ds(r0, c), cols].astype(F32).T
                qf = q_ref[pl.ds(r0, c), cols].astype(F32)
                kf = k_ref[pl.ds(r0, c), cols].astype(F32)
                df = jnp.exp(lg_ref[pl.ds(r0, c), cols])

                def token(ti, carry2, vt=vt, qf=qf, kf=kf, df=df):
                    st, ot = carry2
                    sel = lane == ti
                    pick = lambda a: jnp.sum(jnp.where(sublane == ti, a, 0.0), axis=0, keepdims=True)
                    v_col = jnp.sum(jnp.where(sel, vt, 0.0), axis=1, keepdims=True)
                    st = st * pick(df) + v_col * pick(kf)
                    o_col = jnp.sum(st * pick(qf), axis=1, keepdims=True)
                    return st, jnp.where(sel, o_col, ot)

                st, ot = lax.fori_loop(0, c, token, (st_scr[h], jnp.zeros((HG_DV, c), F32)))
                st_scr[h] = st
                finish(r0, h, ot.T)
            return carry

        lax.fori_loop(0, n_chunks, slow_chunk, 0)

    @pl.when(t == pl.num_programs(1) - 1)
    def _():
        for h in range(HG_HEADS):
            s_out_ref[h] = st_scr[h].T


def _gla_prompt(qs, k, v, og, lg, g_out_row, batch, seq):
    tb = GLA_BLOCK
    stream = pl.BlockSpec((None, tb, D_MODEL), lambda b, t: (b, t, 0))
    shp = (batch, seq, D_MODEL)
    return pl.pallas_call(
        _gla_body,
        out_shape=[jax.ShapeDtypeStruct(shp, BF16),
                   jax.ShapeDtypeStruct((batch, HG_HEADS, HG_DK, HG_DV), F32)],
        grid=(batch, seq // tb),
        in_specs=[stream, stream, stream, stream, stream, _full_spec((1, D_MODEL))],
        out_specs=[stream, pl.BlockSpec((None, HG_HEADS, HG_DK, HG_DV), lambda b, t: (b, 0, 0, 0))],
        scratch_shapes=[pltpu.VMEM((HG_HEADS, HG_DV, HG_DK), F32), pltpu.VMEM((tb, D_MODEL), F32)],
        compiler_params=_params("parallel", "arbitrary"),
        name="gla_chunked",
    )(qs.reshape(shp), k.reshape(shp), v.reshape(shp), og.reshape(shp), lg.reshape(shp), g_out_row)


def _gla_step_body(q_ref, k_ref, v_ref, og_ref, lg_ref, go_ref, s_ref, y_ref, s_out_ref):
    go = go_ref[...]

    def one(bi, carry):
        qt = q_ref[bi].astype(F32).T
        kt = k_ref[bi].astype(F32).T
        dt = jnp.exp(lg_ref[bi]).T
        v = v_ref[bi].astype(F32)
        outs = []
        for h in range(HG_HEADS):
            s = s_ref[bi, h] * dt[:, h:h + 1] + kt[:, h:h + 1] * v[h:h + 1, :]
            s_out_ref[bi, h] = s
            outs.append(jnp.sum(s * qt[:, h:h + 1], axis=0, keepdims=True))
        o = jnp.concatenate(outs, axis=0)
        y_ref[bi] = (_rms(o) * go * og_ref[bi].astype(F32)).astype(y_ref.dtype)
        return carry

    lax.fori_loop(0, q_ref.shape[0], one, 0)


def _gla_step(qs, k, v, og, lg, g_out_heads, state, layer, bb=STEP_SEQS):
    rows = qs.shape[0]
    shp = (rows, HG_HEADS, HG_DK)
    stream = pl.BlockSpec((bb, HG_HEADS, HG_DK), lambda i: (i, 0, 0))
    return pl.pallas_call(
        _gla_step_body,
        out_shape=[jax.ShapeDtypeStruct(shp, F32),
                   jax.ShapeDtypeStruct((rows, HG_HEADS, HG_DK, HG_DV), F32)],
        grid=(rows // bb,),
        in_specs=[stream, stream, stream, stream, stream, _full_spec((HG_HEADS, HG_DV)),
                  pl.BlockSpec((None, bb, HG_HEADS, HG_DK, HG_DV), lambda i: (layer, i, 0, 0, 0))],
        out_specs=[stream, pl.BlockSpec((bb, HG_HEADS, HG_DK, HG_DV), lambda i: (i, 0, 0, 0))],
        compiler_params=_params("parallel"),
        name="gla_step",
    )(qs.reshape(shp), k.reshape(shp), v.reshape(shp), og.reshape(shp), lg.reshape(shp), g_out_heads, state)


def _proj_res_body(a_ref, w_ref, x_ref, gate_ref, o_ref):
    o_ref[...] = x_ref[...] + gate_ref[...] * _dot(a_ref[...].astype(BF16), w_ref[...])


def _proj_res(grp, a, w, layer, x, mods, sub):
    kdim = a.shape[1]
    return pl.pallas_call(
        _proj_res_body,
        out_shape=jax.ShapeDtypeStruct((grp.rows, D_MODEL), F32),
        grid=(grp.rows // grp.tm,),
        in_specs=[_row_spec(grp, kdim),
                  pl.BlockSpec((None, kdim, D_MODEL), lambda i: (layer, 0, 0)),
                  _row_spec(grp, D_MODEL),
                  _mod_spec(grp, mods, sub * 3 + 2)],
        out_specs=_row_spec(grp, D_MODEL),
        compiler_params=_params("parallel"),
        name="proj_residual",
    )(a, w, x, mods)


def _kv_body(x_ref, shift_ref, scale_ref, g_ref, wc_ref, wr_ref, wrr_ref, gl_ref, cos_ref, sin_ref,
             ckv_ref, kr_ref, kcat_ref, ckvt_ref):
    h = _mod_norm(x_ref[...], g_ref[...], scale_ref[...], shift_ref[...]).astype(BF16)
    ckv = _rms(_dot(h, wc_ref[...])) * gl_ref[...]
    kr = _dot(h, wr_ref[...]) * cos_ref[...] + _dot(h, wrr_ref[...]) * sin_ref[...]
    ckv_ref[...] = ckv
    kr_ref[...] = kr
    kcat_ref[:, 0:KV_RANK] = ckv.astype(BF16)
    kcat_ref[:, KV_RANK:QK_LAT] = kr.astype(BF16)
    ckvt_ref[0:KV_RANK, :] = ckv.T.astype(BF16)
    ckvt_ref[KV_RANK:VT_ROWS, :] = jnp.ones((VT_ROWS - KV_RANK, ckvt_ref.shape[1]), BF16)


def _shared_kv(grp, x, kvmods, g_norm_row, w_c, w_r, w_rr, g_latent_row, cos_t, sin_t):
    tps = grp.tiles_per_seq
    return pl.pallas_call(
        _kv_body,
        out_shape=[jax.ShapeDtypeStruct((grp.rows, KV_RANK), F32),
                   jax.ShapeDtypeStruct((grp.rows, QK_ROPE), F32),
                   jax.ShapeDtypeStruct((grp.rows, QK_LAT), BF16),
                   jax.ShapeDtypeStruct((grp.rows // grp.seq, VT_ROWS, grp.seq), BF16)],
        grid=(grp.rows // grp.tm,),
        in_specs=[_row_spec(grp, D_MODEL),
                  _mod_spec(grp, kvmods, 0),
                  _mod_spec(grp, kvmods, 1),
                  _full_spec((1, D_MODEL)),
                  _full_spec(w_c.shape), _full_spec(w_r.shape), _full_spec(w_rr.shape),
                  _full_spec((1, KV_RANK)),
                  _pos_spec(grp, QK_ROPE), _pos_spec(grp, QK_ROPE)],
        out_specs=[_row_spec(grp, KV_RANK), _row_spec(grp, QK_ROPE), _row_spec(grp, QK_LAT),
                   pl.BlockSpec((None, VT_ROWS, grp.tm), lambda i: (i // tps, 0, i % tps))],
        compiler_params=_params("parallel"),
        name="shared_kv",
    )(x, kvmods, kvmods, g_norm_row, w_c, w_r, w_rr, g_latent_row, cos_t, sin_t)


def _mla_q_body(x_ref, shift_ref, scale_ref, g_ref, wdq_ref, gq_ref, wn_ref, wrt_ref, wrrt_ref, wukt_ref,
                cos_ref, sin_ref, q_ref):
    tm = x_ref.shape[0]
    h = _mod_norm(x_ref[...], g_ref[...], scale_ref[...], shift_ref[...]).astype(BF16)
    qc = (_rms(_dot(h, wdq_ref[...])) * gq_ref[...]).astype(BF16)
    qn = _dot(qc, wn_ref[...])
    qrt = (_dot_nt(wrt_ref[...], qc) * cos_ref[...] + _dot_nt(wrrt_ref[...], qc) * sin_ref[...]) * SM_SCALE
    for hd in range(MLA_HEADS):
        cols = slice(hd * tm, (hd + 1) * tm)
        qlt = _dot_nt(wukt_ref[hd], qn[:, hd * QK_NOPE:(hd + 1) * QK_NOPE].astype(BF16)) * SM_SCALE
        q_ref[0:KV_RANK, cols] = qlt.astype(q_ref.dtype)
        q_ref[KV_RANK:QK_LAT, cols] = qrt[hd * QK_ROPE:(hd + 1) * QK_ROPE, :].astype(q_ref.dtype)


def _mla_q(grp, x, mods, g_norm_row, w_dq, g_q_row, w_n, w_rt, w_rrt, w_ukt, cos_t, sin_t, layer_b):
    sub = 1
    nseq = grp.rows // grp.seq
    tps = grp.tiles_per_seq
    width = MLA_HEADS * QK_ROPE
    sel = lambda a: pl.BlockSpec((None,) + a.shape[1:], lambda i: (layer_b,) + (0,) * (a.ndim - 1))
    if grp.per_row:
        pos = pl.BlockSpec((width, grp.tm), lambda i: (0, i))
    else:
        pos = pl.BlockSpec((width, grp.tm), lambda i: (0, i % tps))
    return pl.pallas_call(
        _mla_q_body,
        out_shape=jax.ShapeDtypeStruct((nseq, tps, QK_LAT, MLA_HEADS * grp.tm), BF16),
        grid=(grp.rows // grp.tm,),
        in_specs=[_row_spec(grp, D_MODEL),
                  _mod_spec(grp, mods, sub * 3 + 0),
                  _mod_spec(grp, mods, sub * 3 + 1),
                  _full_spec((1, D_MODEL)),
                  sel(w_dq), sel(g_q_row), sel(w_n), sel(w_rt), sel(w_rrt),
                  _full_spec(w_ukt.shape),
                  pos, pos],
        out_specs=pl.BlockSpec((None, None, QK_LAT, MLA_HEADS * grp.tm), lambda i: (i // tps, i % tps, 0, 0)),
        compiler_params=_params("parallel"),
        name="mla_q",
    )(x, mods, mods, g_norm_row, w_dq, g_q_row, w_n, w_rt, w_rrt, w_ukt, cos_t, sin_t)


def _attn_body(qi_ref, kj_ref, qt_ref, k_ref, vt_ref, o_ref, m_scr, acc_scr, *, tq):
    p = pl.program_id(1)
    i = qi_ref[p]
    j = kj_ref[p]

    @pl.when(j == 0)
    def _():
        m_scr[...] = jnp.full_like(m_scr, -jnp.inf)
        acc_scr[...] = jnp.zeros_like(acc_scr)

    def update(masked):
        s = _dot(k_ref[...], qt_ref[...])
        if masked:
            kpos = lax.broadcasted_iota(jnp.int32, s.shape, 0)
            qpos = lax.broadcasted_iota(jnp.int32, s.shape, 1) & (tq - 1)
            s = jnp.where(kpos <= qpos, s, -jnp.inf)
        m_prev = m_scr[...]
        m_new = jnp.maximum(m_prev, jnp.max(s, axis=0, keepdims=True))
        alpha = jnp.exp(m_prev - m_new)
        pr = jnp.exp(s - m_new).astype(BF16)
        acc_scr[...] = alpha * acc_scr[...] + _dot(vt_ref[...], pr)
        m_scr[...] = m_new

    @pl.when(j < i)
    def _():
        update(False)

    @pl.when(j == i)
    def _():
        update(True)
        out = acc_scr[0:KV_RANK, :] / acc_scr[KV_RANK:KV_RANK + 1, :]
        for hd in range(MLA_HEADS):
            o_ref[:, hd * KV_RANK:(hd + 1) * KV_RANK] = out[:, hd * tq:(hd + 1) * tq].T.astype(o_ref.dtype)


def _attend_prompt(qt, kcat, ckvt, batch, seq, tq):
    nq = seq // tq
    pairs = [(i, j) for i in range(nq) for j in range(i + 1)]
    qi = jnp.asarray(np.array([p[0] for p in pairs], np.int32))
    kj = jnp.asarray(np.array([p[1] for p in pairs], np.int32))
    cols = MLA_HEADS * tq
    return pl.pallas_call(
        functools.partial(_attn_body, tq=tq),
        out_shape=jax.ShapeDtypeStruct((batch, seq, MLA_HEADS * KV_RANK), BF16),
        grid_spec=pltpu.PrefetchScalarGridSpec(
            num_scalar_prefetch=2,
            grid=(batch, len(pairs)),
            in_specs=[pl.BlockSpec((None, None, QK_LAT, cols), lambda b, p, qi, kj: (b, qi[p], 0, 0)),
                      pl.BlockSpec((None, tq, QK_LAT), lambda b, p, qi, kj: (b, kj[p], 0)),
                      pl.BlockSpec((None, VT_ROWS, tq), lambda b, p, qi, kj: (b, 0, kj[p]))],
            out_specs=pl.BlockSpec((None, tq, MLA_HEADS * KV_RANK), lambda b, p, qi, kj: (b, qi[p], 0)),
            scratch_shapes=[pltpu.VMEM((1, cols), F32), pltpu.VMEM((VT_ROWS, cols), F32)],
        ),
        compiler_params=_params("parallel", "arbitrary"),
        name="attend_prompt",
    )(qi, kj, qt, kcat.reshape(batch, seq, QK_LAT), ckvt)


PAGED_CHUNK_PAGES = 32
PAGED_SLOTS = 4


def _paged_body(pt_ref, q_ref, cn_ref, kn_ref, ckv_hbm, krt_hbm, o_ref, cbuf, kbuf, sems, *, n_pages):
    ch = PAGED_CHUNK_PAGES
    nc = n_pages // ch
    ahead = PAGED_SLOTS - 1
    nseq = pl.num_programs(0)
    b = pl.program_id(0)

    def copies(g, slot):
        out = []
        for pg in range(ch):
            page = pt_ref[g * ch + pg]
            dst = pl.ds(pg * PAGE_SIZE, PAGE_SIZE)
            out.append(pltpu.make_async_copy(ckv_hbm.at[page], cbuf.at[slot, dst, :], sems.at[0, slot]))
            out.append(pltpu.make_async_copy(krt_hbm.at[page], kbuf.at[slot, pg], sems.at[1, slot]))
        return out

    def start(g, slot):
        for cp in copies(g, slot):
            cp.start()

    def wait(slot):
        for cp in copies(0, slot):
            cp.wait()

    @pl.when(b == 0)
    def _():
        for c in range(ahead):
            start(c, c)

    q = q_ref[...]
    ql = q[:, 0:KV_RANK]
    qr = q[:, KV_RANK:QK_LAT]
    cn = cn_ref[...]
    kn = kn_ref[...]
    m = jnp.sum(ql * cn, axis=1, keepdims=True) + jnp.sum(qr * kn, axis=1, keepdims=True)
    l = jnp.ones_like(m)
    acc = jnp.broadcast_to(cn, (MLA_HEADS, KV_RANK))
    qlb = ql.astype(BF16)
    qrb = qr.astype(BF16)
    for c in range(nc):
        slot = c % PAGED_SLOTS
        g = b * nc + c
        nxt = (c + ahead) % PAGED_SLOTS
        if c + ahead < nc:
            start(g + ahead, nxt)
        else:
            @pl.when(b + 1 < nseq)
            def _():
                start(g + ahead, nxt)
        wait(slot)
        kc = cbuf[slot].astype(BF16)
        s_rope = [_dot(qrb, kbuf[slot, pg].astype(BF16)) for pg in range(ch)]
        s = _dot_nt(qlb, kc) + jnp.concatenate(s_rope, axis=1)
        m_new = jnp.maximum(m, jnp.max(s, axis=1, keepdims=True))
        alpha = jnp.exp(m - m_new)
        pr = jnp.exp(s - m_new)
        l = alpha * l + jnp.sum(pr, axis=1, keepdims=True)
        acc = alpha * acc + _dot(pr.astype(BF16), kc)
        m = m_new
    o_ref[...] = acc / l


def _attend_paged(q, ckv_new, krope_new, cache_ckv, cache_krope_t, page_table):
    nseq, n_pages = page_table.shape
    ch = PAGED_CHUNK_PAGES
    assert n_pages % (PAGED_SLOTS * ch) == 0
    per_seq = lambda *tail: pl.BlockSpec((None,) + tail, lambda i, pt: (i,) + (0,) * len(tail))
    return pl.pallas_call(
        functools.partial(_paged_body, n_pages=n_pages),
        out_shape=jax.ShapeDtypeStruct((nseq, MLA_HEADS, KV_RANK), F32),
        grid_spec=pltpu.PrefetchScalarGridSpec(
            num_scalar_prefetch=1,
            grid=(nseq,),
            in_specs=[per_seq(MLA_HEADS, QK_LAT), per_seq(1, KV_RANK), per_seq(1, QK_ROPE),
                      pl.BlockSpec(memory_space=pl.ANY), pl.BlockSpec(memory_space=pl.ANY)],
            out_specs=per_seq(MLA_HEADS, KV_RANK),
            scratch_shapes=[pltpu.VMEM((PAGED_SLOTS, ch * PAGE_SIZE, KV_RANK), F32),
                            pltpu.VMEM((PAGED_SLOTS, ch, QK_ROPE, PAGE_SIZE), F32),
                            pltpu.SemaphoreType.DMA((2, PAGED_SLOTS))],
        ),
        compiler_params=_params("arbitrary"),
        name="attend_paged",
    )(page_table.reshape(-1), q, ckv_new.reshape(nseq, 1, KV_RANK), krope_new.reshape(nseq, 1, QK_ROPE),
      cache_ckv, cache_krope_t)


def _attn_out_body(o_ref, wuv_ref, wo_ref, x_ref, gate_ref, out_ref):
    parts = [_dot(o_ref[:, hd * KV_RANK:(hd + 1) * KV_RANK], wuv_ref[hd]).astype(BF16)
             for hd in range(MLA_HEADS)]
    o = jnp.concatenate(parts, axis=1)
    out_ref[...] = x_ref[...] + gate_ref[...] * _dot(o, wo_ref[...])


def _attn_out(grp, o_lat, w_uv_t, w_out, layer_b, x, mods, sub):
    width = MLA_HEADS * KV_RANK
    return pl.pallas_call(
        _attn_out_body,
        out_shape=jax.ShapeDtypeStruct((grp.rows, D_MODEL), F32),
        grid=(grp.rows // grp.tm,),
        in_specs=[_row_spec(grp, width),
                  _full_spec(w_uv_t.shape),
                  pl.BlockSpec((None, MLA_HEADS * V_HEAD, D_MODEL), lambda i: (layer_b, 0, 0)),
                  _row_spec(grp, D_MODEL),
                  _mod_spec(grp, mods, sub * 3 + 2)],
        out_specs=_row_spec(grp, D_MODEL),
        compiler_params=_params("parallel"),
        name="attn_out",
    )(o_lat, w_uv_t, w_out, x, mods)


def _rope_tables(pos):
    half = QK_ROPE // 2
    inv_freq = ROPE_THETA ** (-2.0 * jnp.arange(half, dtype=F32) / QK_ROPE)
    ang = pos.astype(F32)[:, None] * inv_freq[None, :]
    cos, sin = jnp.cos(ang), jnp.sin(ang)
    return jnp.concatenate([cos, cos], axis=-1), jnp.concatenate([-sin, sin], axis=-1)


def _swap_halves(w):
    half = QK_ROPE // 2
    return jnp.concatenate([w[..., half:], w[..., :half]], axis=-1)


def kernel(x_prompt, x_sample, c_prompt, c_sample, state_hgrn, cache_ckv, cache_krope, page_table, w_ada, b_ada, g_norm, w_ffn_gu, w_ffn_down, hg_w_in, hg_lb, hg_g_out, hg_w_out, kv_w_ada, kv_b_ada, kv_g_norm, kv_w_down, kv_g_latent, kv_w_uk, kv_w_uv, q_w_down, q_g_norm, q_w_up, attn_w_out, g_final):
    batch, seq, d = x_prompt.shape
    nseq = x_sample.shape[0]
    n_b = DEPTH - N_A_LAYERS

    w_in = hg_w_in.astype(BF16)
    w_ho = hg_w_out.astype(BF16)
    w_ao = attn_w_out.astype(BF16)
    w_kc = kv_w_down[:, :KV_RANK].astype(BF16)
    w_kr = kv_w_down[:, KV_RANK:].astype(BF16)
    w_krr = _swap_halves(kv_w_down[:, KV_RANK:]).astype(BF16)
    w_dq = q_w_down.astype(BF16)
    w_qn = q_w_up[..., :QK_NOPE].reshape(n_b, Q_RANK, MLA_HEADS * QK_NOPE).astype(BF16)
    w_qr = q_w_up[..., QK_NOPE:].reshape(n_b, Q_RANK, MLA_HEADS * QK_ROPE)
    w_qrr = _swap_halves(q_w_up[..., QK_NOPE:]).reshape(n_b, Q_RANK, MLA_HEADS * QK_ROPE)
    w_qrt = w_qr.transpose(0, 2, 1).astype(BF16)
    w_qrrt = w_qrr.transpose(0, 2, 1).astype(BF16)
    w_ukt = kv_w_uk.transpose(1, 0, 2).astype(BF16)
    w_uv_t = kv_w_uv.transpose(1, 0, 2).astype(BF16)
    cache_krope_t = cache_krope.transpose(0, 2, 1)
    g_q_rows = q_g_norm.reshape(n_b, 1, Q_RANK)
    g_out_rows = hg_g_out.reshape(N_A_LAYERS, 1, D_MODEL)

    pad = 8
    c_all = jnp.concatenate([c_prompt, jnp.zeros((pad - batch, d), F32), c_sample], axis=0)
    mods_all = [_ada(c_all, w_ada, b_ada.reshape(DEPTH, 1, -1), l) for l in range(DEPTH)]
    kvm_all = _ada(c_all, kv_w_ada[None], kv_b_ada.reshape(1, 1, -1), 0)

    def trunk(grp, x, mods_l, kvmods, pos, mix, attend, stream_dtype):
        grp_half = grp._replace(tm=min(grp.tm, WIDE_TM))
        grp_q = grp._replace(tm=min(grp.tm, ATTN_TQ))
        cos_k, sin_k = _rope_tables(pos)
        cos_qt, sin_qt = jnp.tile(cos_k, (1, MLA_HEADS)).T, jnp.tile(sin_k, (1, MLA_HEADS)).T
        states = []
        ckv = krope = kcat = ckvt = None
        for l in range(DEPTH):
            mods = mods_l[l]
            gn = lambda j: g_norm[l, j].reshape(1, d)
            if l == N_A_LAYERS:
                ckv, krope, kcat, ckvt = _shared_kv(grp, x, kvmods, kv_g_norm.reshape(1, d), w_kc, w_kr, w_krr,
                                                    kv_g_latent.reshape(1, KV_RANK), cos_k, sin_k)
            x = _ffn(grp, x, mods, 0, gn(0), w_ffn_gu, w_ffn_down, l, 0)
            if l < N_A_LAYERS:
                qs, k, v, og, lg = _hgrn_in(grp_half, x, mods, gn(1), w_in, hg_lb, l, stream_dtype)
                y, s = mix(l, qs, k, v, og, lg)
                states.append(s)
                x = _proj_res(grp, y, w_ho, l, x, mods, 1)
            else:
                lb_ = l - N_A_LAYERS
                qt = _mla_q(grp_q, x, mods, gn(1), w_dq, g_q_rows, w_qn, w_qrt, w_qrrt, w_ukt, cos_qt, sin_qt,
                            lb_)
                o_lat = attend(qt, ckv, krope, kcat, ckvt)
                x = _attn_out(grp, o_lat, w_uv_t, w_ao, lb_, x, mods, 1)
            x = _ffn(grp, x, mods, 2, gn(2), w_ffn_gu, w_ffn_down, l, 1,
                     g_final=g_final.reshape(1, d) if l == DEPTH - 1 else None)
        return x, jnp.stack(states), ckv, krope

    grp_p = Group(rows=batch * seq, seq=seq, tm=PROMPT_TM, per_row=False)
    mods_p = [m[:pad].reshape(pad, 1, -1) for m in mods_all]
    kvm_p = kvm_all[:pad].reshape(pad, 1, -1)

    def mix_p(l, qs, k, v, og, lg):
        y, s = _gla_prompt(qs, k, v, og, lg, g_out_rows[l], batch, seq)
        return y.reshape(batch * seq, d), s

    def attend_p(qt, ckv, krope, kcat, ckvt):
        o = _attend_prompt(qt, kcat, ckvt, batch, seq, ATTN_TQ)
        return o.reshape(batch * seq, MLA_HEADS * KV_RANK)

    y_p, st_p, ckv_p, kr_p = trunk(grp_p, x_prompt.reshape(batch * seq, d), mods_p, kvm_p,
                                   jnp.arange(seq), mix_p, attend_p, BF16)

    assert nseq == MIN_TM
    grp_s = Group(rows=nseq, seq=nseq, tm=nseq, per_row=True)
    mods_s = [m[pad:] for m in mods_all]
    kvm_s = kvm_all[pad:]

    def mix_s(l, qs, k, v, og, lg):
        y, s = _gla_step(qs, k, v, og, lg, hg_g_out[l].reshape(HG_HEADS, HG_DV), state_hgrn, l)
        return y.reshape(nseq, d), s

    def attend_s(qt, ckv, krope, kcat, ckvt):
        qf = qt.reshape(QK_LAT, MLA_HEADS, nseq).transpose(2, 1, 0).astype(F32)
        o = _attend_paged(qf, ckv, krope, cache_ckv, cache_krope_t, page_table)
        return o.reshape(nseq, MLA_HEADS * KV_RANK).astype(BF16)

    y_s, st_s, ckv_s, kr_s = trunk(grp_s, x_sample.reshape(nseq, d), mods_s, kvm_s,
                                   jnp.full((nseq,), PAST_LEN), mix_s, attend_s, F32)

    return (y_p.reshape(batch, seq, d), y_s.reshape(nseq, 1, d),
            st_p.astype(state_hgrn.dtype), st_s.astype(state_hgrn.dtype),
            ckv_p.reshape(batch, seq, KV_RANK), kr_p.reshape(batch, seq, QK_ROPE),
            ckv_s.reshape(nseq, 1, KV_RANK), kr_s.reshape(nseq, 1, QK_ROPE))
```

```python
import functools
from typing import NamedTuple

import jax
import jax.numpy as jnp
import numpy as np
from jax import lax
from jax.experimental import pallas as pl
from jax.experimental.pallas import tpu as pltpu

D_MODEL = 1024
DEPTH = 2
N_A_LAYERS = DEPTH // 2
PAST_LEN = 16384
PAGE_SIZE = 128
D_FF = 2816
HG_HEADS = 8
HG_DK = D_MODEL // HG_HEADS
HG_DV = D_MODEL // HG_HEADS
MLA_HEADS = 8
QK_NOPE = 128
QK_ROPE = 64
V_HEAD = 128
KV_RANK = 256
Q_RANK = 384
QK_LAT = KV_RANK + QK_ROPE
ROPE_THETA = 10000.0
EPS = 1e-6
SM_SCALE = (QK_NOPE + QK_ROPE) ** -0.5

BF16 = jnp.bfloat16
F32 = jnp.float32

V7X_VMEM_BYTES = 64 * 1024 * 1024
VMEM_LIMIT = V7X_VMEM_BYTES * 3 // 4

PROMPT_TM = 1024
WIDE_TM = 512
MIN_TM = 128
ADA_TN = 1024
STEP_SEQS = 8

GLA_CHUNK = 64
GLA_SUB = 16
GLA_BLOCK = 512
GLA_SAFE_DECAY = 60.0

ATTN_TQ = 512
VT_ROWS = KV_RANK + 16

NT_DIMS = (((1,), (1,)), ((), ()))


class Group(NamedTuple):
    rows: int
    seq: int
    tm: int
    per_row: bool

    @property
    def tiles_per_seq(self):
        return self.seq // self.tm


def _params(*sem):
    return pltpu.CompilerParams(dimension_semantics=sem, vmem_limit_bytes=VMEM_LIMIT)


def _dot(a, b):
    return jnp.dot(a, b, preferred_element_type=F32)


def _dot_nt(a, b):
    return lax.dot_general(a, b, NT_DIMS, preferred_element_type=F32)


def _sigmoid(x):
    return 1.0 / (1.0 + jnp.exp(-x))


def _rms(x):
    return x * lax.rsqrt(jnp.mean(x * x, axis=-1, keepdims=True) + EPS)


def _mod_norm(x, g, scale, shift):
    return (_rms(x) * g) * (1.0 + scale) + shift


def _mod_spec(grp, mods, col):
    if grp.per_row:
        return pl.BlockSpec((grp.tm, D_MODEL), lambda *g: (g[0], col))
    tps = grp.tiles_per_seq
    return pl.BlockSpec((None, 1, D_MODEL), lambda *g: (g[0] // tps, 0, col))


def _pos_spec(grp, width):
    if grp.per_row:
        return pl.BlockSpec((grp.tm, width), lambda *g: (g[0], 0))
    tps = grp.tiles_per_seq
    return pl.BlockSpec((grp.tm, width), lambda *g: (g[0] % tps, 0))


def _row_spec(grp, width):
    return pl.BlockSpec((grp.tm, width), lambda *g: (g[0], 0))


def _full_spec(shape):
    nd = len(shape)
    return pl.BlockSpec(shape, lambda *g: (0,) * nd)


def _ada_body(c_ref, w_ref, b_ref, o_ref):
    c = c_ref[...]
    sc = (c * _sigmoid(c)).astype(BF16)
    o_ref[...] = _dot(sc, w_ref[...].astype(BF16)) + b_ref[...]


def _ada(c_all, w, b, layer, tn=ADA_TN):
    rows = c_all.shape[0]
    n = w.shape[-1]
    return pl.pallas_call(
        _ada_body,
        out_shape=jax.ShapeDtypeStruct((rows, n), F32),
        grid=(n // tn,),
        in_specs=[
            _full_spec((rows, D_MODEL)),
            pl.BlockSpec((None, D_MODEL, tn), lambda j: (layer, 0, j)),
            pl.BlockSpec((None, 1, tn), lambda j: (layer, 0, j)),
        ],
        out_specs=pl.BlockSpec((rows, tn), lambda j: (0, j)),
        compiler_params=_params("arbitrary"),
        name="ada_proj",
    )(c_all, w, b)


FFN_TF = 256
FFN_GROUP = 3
FFN_ROW_PARTS = 2


def _ffn_body(x_ref, shift_ref, scale_ref, gate_ref, g_ref, wgu_hbm, wd_hbm, *rest, layer, half, final_norm):
    if final_norm:
        gf_ref, o_ref, h_scr, acc_scr, wg_buf, wu_buf, wd_buf, sems = rest
    else:
        o_ref, h_scr, acc_scr, wg_buf, wu_buf, wd_buf, sems = rest
    tf = FFN_TF
    nf = D_FF // tf
    groups = [list(range(s, min(s + FFN_GROUP, nf))) for s in range(0, nf, FFN_GROUP)]
    ng = len(groups)
    assert ng % 2 == 0
    i = pl.program_id(0)
    ni = pl.num_programs(0)

    def copies(g, slot):
        out = []
        for t, j in enumerate(groups[g]):
            out += [
                pltpu.make_async_copy(wgu_hbm.at[layer, half, :, pl.ds(j * tf, tf)], wg_buf.at[slot, t],
                                      sems.at[0, slot]),
                pltpu.make_async_copy(wgu_hbm.at[layer, half, :, pl.ds((nf + j) * tf, tf)], wu_buf.at[slot, t],
                                      sems.at[1, slot]),
                pltpu.make_async_copy(wd_hbm.at[layer, half, pl.ds(j * tf, tf), :], wd_buf.at[slot, t],
                                      sems.at[2, slot]),
            ]
        return out

    def start(g, slot):
        for cp in copies(g, slot):
            cp.start()

    def wait(g, slot):
        for cp in copies(g, slot):
            cp.wait()

    @pl.when(i == 0)
    def _():
        start(0, 0)

    tm = x_ref.shape[0]
    n_parts = FFN_ROW_PARTS if tm % (FFN_ROW_PARTS * 16) == 0 and tm > MIN_TM else 1
    parts = [slice(p * tm // n_parts, (p + 1) * tm // n_parts) for p in range(n_parts)]
    rows_of = lambda ref, p: ref[p, :] if ref.shape[0] == tm else ref[...]

    for g in range(ng):
        slot = g % 2
        if g + 1 < ng:
            start(g + 1, 1 - slot)
        else:
            @pl.when(i + 1 < ni)
            def _():
                start(0, 1 - slot)
        wait(g, slot)
        for t, j in enumerate(groups[g]):
            wg = wg_buf[slot, t].astype(BF16)
            wu = wu_buf[slot, t].astype(BF16)
            wd = wd_buf[slot, t].astype(BF16)
            for p in parts:
                if j == 0:
                    h_scr[p, :] = _mod_norm(x_ref[p, :], g_ref[...], rows_of(scale_ref, p),
                                            rows_of(shift_ref, p)).astype(BF16)
                h = h_scr[p, :]
                a = _dot(h, wg)
                b = _dot(h, wu)
                act = ((a * _sigmoid(a)) * b).astype(BF16)
                part = _dot(act, wd)
                if j == 0:
                    acc_scr[p, :] = part
                else:
                    acc_scr[p, :] += part

    y = x_ref[...] + (0.5 * gate_ref[...]) * acc_scr[...]
    if final_norm:
        y = _rms(y) * gf_ref[...]
    o_ref[...] = y


def _ffn(grp, x, mods, sub, g_norm_row, w_gu, w_down, layer, half, g_final=None):
    tf = FFN_TF
    final_norm = g_final is not None
    in_specs = [
        _row_spec(grp, D_MODEL),
        _mod_spec(grp, mods, sub * 3 + 0),
        _mod_spec(grp, mods, sub * 3 + 1),
        _mod_spec(grp, mods, sub * 3 + 2),
        _full_spec((1, D_MODEL)),
        pl.BlockSpec(memory_space=pl.ANY),
        pl.BlockSpec(memory_space=pl.ANY),
    ]
    args = [x, mods, mods, mods, g_norm_row, w_gu, w_down]
    if final_norm:
        in_specs.append(_full_spec((1, D_MODEL)))
        args.append(g_final)
    return pl.pallas_call(
        functools.partial(_ffn_body, layer=layer, half=half, final_norm=final_norm),
        out_shape=jax.ShapeDtypeStruct((grp.rows, D_MODEL), F32),
        grid=(grp.rows // grp.tm,),
        in_specs=in_specs,
        out_specs=_row_spec(grp, D_MODEL),
        scratch_shapes=[pltpu.VMEM((grp.tm, D_MODEL), BF16), pltpu.VMEM((grp.tm, D_MODEL), F32),
                        pltpu.VMEM((2, FFN_GROUP, D_MODEL, tf), F32), pltpu.VMEM((2, FFN_GROUP, D_MODEL, tf), F32),
                        pltpu.VMEM((2, FFN_GROUP, tf, D_MODEL), F32), pltpu.SemaphoreType.DMA((3, 2))],
        compiler_params=_params("arbitrary"),
        name="ffn_half",
    )(*args)


def _hgrn_in_body(x_ref, shift_ref, scale_ref, g_ref, w_ref, lb_ref, qs_ref, k_ref, v_ref, og_ref, lg_ref,
                  *, layer):
    d = D_MODEL
    h = _mod_norm(x_ref[...], g_ref[...], scale_ref[...], shift_ref[...]).astype(BF16)
    lbp = lb_ref[...]
    e = jnp.exp(lbp - jnp.max(lbp, axis=0, keepdims=True))
    lb = jnp.sum(e[: layer + 1], axis=0, keepdims=True) / jnp.sum(e, axis=0, keepdims=True)

    f = _dot(h, w_ref[:, d:2 * d])
    q = _dot(h, w_ref[:, 0:d])
    g = _dot(h, w_ref[:, 3 * d:4 * d])
    v_ref[...] = _dot(h, w_ref[:, 2 * d:3 * d]).astype(v_ref.dtype)
    log_sig = jnp.minimum(f, 0.0) - jnp.log(1.0 + jnp.exp(-jnp.abs(f)))
    a = jnp.log(lb)
    b = jnp.log(1.0 - lb) + log_sig
    lg_ref[...] = jnp.maximum(a, b) + jnp.log(1.0 + jnp.exp(-jnp.abs(a - b)))
    k_ref[...] = ((1.0 - lb) / (1.0 + jnp.exp(f))).astype(k_ref.dtype)
    qs_ref[...] = (q * _sigmoid(q)).astype(qs_ref.dtype)
    og_ref[...] = (g * _sigmoid(g)).astype(og_ref.dtype)


def _hgrn_in(grp, x, mods, g_norm_row, w_in, hg_lb, layer, stream_dtype):
    sub = 1
    outs = ([jax.ShapeDtypeStruct((grp.rows, D_MODEL), stream_dtype)] * 4
            + [jax.ShapeDtypeStruct((grp.rows, D_MODEL), F32)])
    return pl.pallas_call(
        functools.partial(_hgrn_in_body, layer=layer),
        out_shape=outs,
        grid=(grp.rows // grp.tm,),
        in_specs=[
            _row_spec(grp, D_MODEL),
            _mod_spec(grp, mods, sub * 3 + 0),
            _mod_spec(grp, mods, sub * 3 + 1),
            _full_spec((1, D_MODEL)),
            pl.BlockSpec((None, D_MODEL, 4 * D_MODEL), lambda i: (layer, 0, 0)),
            _full_spec(hg_lb.shape),
        ],
        out_specs=[_row_spec(grp, D_MODEL)] * 5,
        compiler_params=_params("parallel"),
        name="hgrn_in",
    )(x, mods, mods, g_norm_row, w_in, hg_lb)


def _segmented_cumsum(x, seg):
    row = lax.broadcasted_iota(jnp.int32, x.shape, 0) & (seg - 1)
    s = 1
    while s < seg:
        x = x + jnp.where(row >= s, pltpu.roll(x, s, 0), 0.0)
        s *= 2
    return x


def _gla_body(q_ref, k_ref, v_ref, og_ref, lg_ref, go_ref, y_ref, s_out_ref, st_scr, b_scr):
    c, sub = GLA_CHUNK, GLA_SUB
    t = pl.program_id(1)

    @pl.when(t == 0)
    def _():
        st_scr[...] = jnp.zeros_like(st_scr)

    b_scr[...] = _segmented_cumsum(lg_ref[...], c)
    tri = lax.broadcasted_iota(jnp.int32, (c, c), 1) <= lax.broadcasted_iota(jnp.int32, (c, c), 0)

    def finish(r0, h, o):
        cols = slice(h * HG_DV, (h + 1) * HG_DV)
        y = _rms(o) * go_ref[:, cols] * og_ref[pl.ds(r0, c), cols].astype(F32)
        y_ref[pl.ds(r0, c), cols] = y.astype(y_ref.dtype)

    n_chunks = q_ref.shape[0] // c

    def chunk_refs(r0):
        b_all = b_scr[pl.ds(r0, c), :]
        return b_all, [jnp.zeros((1, D_MODEL), F32)] + [b_all[i * sub - 1:i * sub, :] for i in range(1, c // sub)]

    worst = jnp.zeros((1, D_MODEL), F32)
    for ci in range(n_chunks):
        b_all, ref_rows = chunk_refs(ci * c)
        for i in range(c // sub):
            worst = jnp.maximum(worst, ref_rows[i] - b_all[(i + 1) * sub - 1:(i + 1) * sub, :])
    safe = jnp.max(worst) <= GLA_SAFE_DECAY

    @pl.when(safe)
    def _():
        for ci in range(n_chunks):
            r0 = ci * c
            b_all, ref_rows = chunk_refs(r0)
            stage = []
            for h in range(HG_HEADS):
                cols = slice(h * HG_DK, (h + 1) * HG_DK)
                b = b_all[:, cols]
                b_last = b[c - 1:c, :]
                refs = [r[:, cols] for r in ref_rows]
                q = q_ref[pl.ds(r0, c), cols].astype(F32)
                k = k_ref[pl.ds(r0, c), cols].astype(F32)
                v = v_ref[pl.ds(r0, c), cols]
                st = st_scr[h]
                o_inter = _dot_nt((q * jnp.exp(b)).astype(BF16), st.astype(BF16))
                rows = []
                for i in range(c // sub):
                    qi = (q[i * sub:(i + 1) * sub] * jnp.exp(b[i * sub:(i + 1) * sub] - refs[i])).astype(BF16)
                    ki = (k * jnp.exp(jnp.minimum(refs[i] - b, GLA_SAFE_DECAY))).astype(BF16)
                    rows.append(_dot_nt(qi, ki))
                kd = (k * jnp.exp(b_last - b)).astype(BF16)
                vt = v.astype(F32).T.astype(BF16)
                st_scr[h] = st * jnp.exp(b_last) + _dot(vt, kd)
                stage.append((o_inter, rows, v))
            for h, (o_inter, rows, v) in enumerate(stage):
                att = jnp.where(tri, jnp.concatenate(rows, axis=0), 0.0)
                finish(r0, h, o_inter + _dot(att.astype(BF16), v))

    @pl.when(jnp.logical_not(safe))
    def _():
        lane = lax.broadcasted_iota(jnp.int32, (HG_DV, c), 1)
        sublane = lax.broadcasted_iota(jnp.int32, (c, HG_DK), 0)

        def slow_chunk(ci, carry):
            r0 = pl.multiple_of(ci * c, c)
            for h in range(HG_HEADS):
                cols = slice(h * HG_DK, (h + 1) * HG_DK)
                vt = v_ref[pl.ds(r0, c), cols].astype(F32).T
                qf = q_ref[pl.ds(r0, c), cols].astype(F32)
                kf = k_ref[pl.ds(r0, c), cols].astype(F32)
                df = jnp.exp(lg_ref[pl.ds(r0, c), cols])

                def token(ti, carry2, vt=vt, qf=qf, kf=kf, df=df):
                    st, ot = carry2
                    sel = lane == ti
                    pick = lambda a: jnp.sum(jnp.where(sublane == ti, a, 0.0), axis=0, keepdims=True)
                    v_col = jnp.sum(jnp.where(sel, vt, 0.0), axis=1, keepdims=True)
                    st = st * pick(df) + v_col * pick(kf)
                    o_col = jnp.sum(st * pick(qf), axis=1, keepdims=True)
                    return st, jnp.where(sel, o_col, ot)

                st, ot = lax.fori_loop(0, c, token, (st_scr[h], jnp.zeros((HG_DV, c), F32)))
                st_scr[h] = st
                finish(r0, h, ot.T)
            return carry

        lax.fori_loop(0, n_chunks, slow_chunk, 0)

    @pl.when(t == pl.num_programs(1) - 1)
    def _():
        for h in range(HG_HEADS):
            s_out_ref[h] = st_scr[h].T


def _gla_prompt(qs, k, v, og, lg, g_out_row, batch, seq):
    tb = GLA_BLOCK
    stream = pl.BlockSpec((None, tb, D_MODEL), lambda b, t: (b, t, 0))
    shp = (batch, seq, D_MODEL)
    return pl.pallas_call(
        _gla_body,
        out_shape=[jax.ShapeDtypeStruct(shp, BF16),
                   jax.ShapeDtypeStruct((batch, HG_HEADS, HG_DK, HG_DV), F32)],
        grid=(batch, seq // tb),
        in_specs=[stream, stream, stream, stream, stream, _full_spec((1, D_MODEL))],
        out_specs=[stream, pl.BlockSpec((None, HG_HEADS, HG_DK, HG_DV), lambda b, t: (b, 0, 0, 0))],
        scratch_shapes=[pltpu.VMEM((HG_HEADS, HG_DV, HG_DK), F32), pltpu.VMEM((tb, D_MODEL), F32)],
        compiler_params=_params("parallel", "arbitrary"),
        name="gla_chunked",
    )(qs.reshape(shp), k.reshape(shp), v.reshape(shp), og.reshape(shp), lg.reshape(shp), g_out_row)


def _gla_step_body(q_ref, k_ref, v_ref, og_ref, lg_ref, go_ref, s_ref, y_ref, s_out_ref):
    go = go_ref[...]

    def one(bi, carry):
        qt = q_ref[bi].astype(F32).T
        kt = k_ref[bi].astype(F32).T
        dt = jnp.exp(lg_ref[bi]).T
        v = v_ref[bi].astype(F32)
        outs = []
        for h in range(HG_HEADS):
            s = s_ref[bi, h] * dt[:, h:h + 1] + kt[:, h:h + 1] * v[h:h + 1, :]
            s_out_ref[bi, h] = s
            outs.append(jnp.sum(s * qt[:, h:h + 1], axis=0, keepdims=True))
        o = jnp.concatenate(outs, axis=0)
        y_ref[bi] = (_rms(o) * go * og_ref[bi].astype(F32)).astype(y_ref.dtype)
        return carry

    lax.fori_loop(0, q_ref.shape[0], one, 0)


def _gla_step(qs, k, v, og, lg, g_out_heads, state, layer, bb=STEP_SEQS):
    rows = qs.shape[0]
    shp = (rows, HG_HEADS, HG_DK)
    stream = pl.BlockSpec((bb, HG_HEADS, HG_DK), lambda i: (i, 0, 0))
    return pl.pallas_call(
        _gla_step_body,
        out_shape=[jax.ShapeDtypeStruct(shp, F32),
                   jax.ShapeDtypeStruct((rows, HG_HEADS, HG_DK, HG_DV), F32)],
        grid=(rows // bb,),
        in_specs=[stream, stream, stream, stream, stream, _full_spec((HG_HEADS, HG_DV)),
                  pl.BlockSpec((None, bb, HG_HEADS, HG_DK, HG_DV), lambda i: (layer, i, 0, 0, 0))],
        out_specs=[stream, pl.BlockSpec((bb, HG_HEADS, HG_DK, HG_DV), lambda i: (i, 0, 0, 0))],
        compiler_params=_params("parallel"),
        name="gla_step",
    )(qs.reshape(shp), k.reshape(shp), v.reshape(shp), og.reshape(shp), lg.reshape(shp), g_out_heads, state)


def _proj_res_body(a_ref, w_ref, x_ref, gate_ref, o_ref):
    o_ref[...] = x_ref[...] + gate_ref[...] * _dot(a_ref[...].astype(BF16), w_ref[...])


def _proj_res(grp, a, w, layer, x, mods, sub):
    kdim = a.shape[1]
    return pl.pallas_call(
        _proj_res_body,
        out_shape=jax.ShapeDtypeStruct((grp.rows, D_MODEL), F32),
        grid=(grp.rows // grp.tm,),
        in_specs=[_row_spec(grp, kdim),
                  pl.BlockSpec((None, kdim, D_MODEL), lambda i: (layer, 0, 0)),
                  _row_spec(grp, D_MODEL),
                  _mod_spec(grp, mods, sub * 3 + 2)],
        out_specs=_row_spec(grp, D_MODEL),
        compiler_params=_params("parallel"),
        name="proj_residual",
    )(a, w, x, mods)


def _kv_body(x_ref, shift_ref, scale_ref, g_ref, wc_ref, wr_ref, wrr_ref, gl_ref, cos_ref, sin_ref,
             ckv_ref, kr_ref, kcat_ref, ckvt_ref):
    h = _mod_norm(x_ref[...], g_ref[...], scale_ref[...], shift_ref[...]).astype(BF16)
    ckv = _rms(_dot(h, wc_ref[...])) * gl_ref[...]
    kr = _dot(h, wr_ref[...]) * cos_ref[...] + _dot(h, wrr_ref[...]) * sin_ref[...]
    ckv_ref[...] = ckv
    kr_ref[...] = kr
    kcat_ref[:, 0:KV_RANK] = ckv.astype(BF16)
    kcat_ref[:, KV_RANK:QK_LAT] = kr.astype(BF16)
    ckvt_ref[0:KV_RANK, :] = ckv.T.astype(BF16)
    ckvt_ref[KV_RANK:VT_ROWS, :] = jnp.ones((VT_ROWS - KV_RANK, ckvt_ref.shape[1]), BF16)


def _shared_kv(grp, x, kvmods, g_norm_row, w_c, w_r, w_rr, g_latent_row, cos_t, sin_t):
    tps = grp.tiles_per_seq
    return pl.pallas_call(
        _kv_body,
        out_shape=[jax.ShapeDtypeStruct((grp.rows, KV_RANK), F32),
                   jax.ShapeDtypeStruct((grp.rows, QK_ROPE), F32),
                   jax.ShapeDtypeStruct((grp.rows, QK_LAT), BF16),
                   jax.ShapeDtypeStruct((grp.rows // grp.seq, VT_ROWS, grp.seq), BF16)],
        grid=(grp.rows // grp.tm,),
        in_specs=[_row_spec(grp, D_MODEL),
                  _mod_spec(grp, kvmods, 0),
                  _mod_spec(grp, kvmods, 1),
                  _full_spec((1, D_MODEL)),
                  _full_spec(w_c.shape), _full_spec(w_r.shape), _full_spec(w_rr.shape),
                  _full_spec((1, KV_RANK)),
                  _pos_spec(grp, QK_ROPE), _pos_spec(grp, QK_ROPE)],
        out_specs=[_row_spec(grp, KV_RANK), _row_spec(grp, QK_ROPE), _row_spec(grp, QK_LAT),
                   pl.BlockSpec((None, VT_ROWS, grp.tm), lambda i: (i // tps, 0, i % tps))],
        compiler_params=_params("parallel"),
        name="shared_kv",
    )(x, kvmods, kvmods, g_norm_row, w_c, w_r, w_rr, g_latent_row, cos_t, sin_t)


def _mla_q_body(x_ref, shift_ref, scale_ref, g_ref, wdq_ref, gq_ref, wn_ref, wrt_ref, wrrt_ref, wukt_ref,
                cos_ref, sin_ref, q_ref):
    tm = x_ref.shape[0]
    h = _mod_norm(x_ref[...], g_ref[...], scale_ref[...], shift_ref[...]).astype(BF16)
    qc = (_rms(_dot(h, wdq_ref[...])) * gq_ref[...]).astype(BF16)
    qn = _dot(qc, wn_ref[...])
    qrt = (_dot_nt(wrt_ref[...], qc) * cos_ref[...] + _dot_nt(wrrt_ref[...], qc) * sin_ref[...]) * SM_SCALE
    for hd in range(MLA_HEADS):
        cols = slice(hd * tm, (hd + 1) * tm)
        qlt = _dot_nt(wukt_ref[hd], qn[:, hd * QK_NOPE:(hd + 1) * QK_NOPE].astype(BF16)) * SM_SCALE
        q_ref[0:KV_RANK, cols] = qlt.astype(q_ref.dtype)
        q_ref[KV_RANK:QK_LAT, cols] = qrt[hd * QK_ROPE:(hd + 1) * QK_ROPE, :].astype(q_ref.dtype)


def _mla_q(grp, x, mods, g_norm_row, w_dq, g_q_row, w_n, w_rt, w_rrt, w_ukt, cos_t, sin_t, layer_b):
    sub = 1
    nseq = grp.rows // grp.seq
    tps = grp.tiles_per_seq
    width = MLA_HEADS * QK_ROPE
    sel = lambda a: pl.BlockSpec((None,) + a.shape[1:], lambda i: (layer_b,) + (0,) * (a.ndim - 1))
    if grp.per_row:
        pos = pl.BlockSpec((width, grp.tm), lambda i: (0, i))
    else:
        pos = pl.BlockSpec((width, grp.tm), lambda i: (0, i % tps))
    return pl.pallas_call(
        _mla_q_body,
        out_shape=jax.ShapeDtypeStruct((nseq, tps, QK_LAT, MLA_HEADS * grp.tm), BF16),
        grid=(grp.rows // grp.tm,),
        in_specs=[_row_spec(grp, D_MODEL),
                  _mod_spec(grp, mods, sub * 3 + 0),
                  _mod_spec(grp, mods, sub * 3 + 1),
                  _full_spec((1, D_MODEL)),
                  sel(w_dq), sel(g_q_row), sel(w_n), sel(w_rt), sel(w_rrt),
                  _full_spec(w_ukt.shape),
                  pos, pos],
        out_specs=pl.BlockSpec((None, None, QK_LAT, MLA_HEADS * grp.tm), lambda i: (i // tps, i % tps, 0, 0)),
        compiler_params=_params("parallel"),
        name="mla_q",
    )(x, mods, mods, g_norm_row, w_dq, g_q_row, w_n, w_rt, w_rrt, w_ukt, cos_t, sin_t)


def _attn_body(qi_ref, kj_ref, qt_ref, k_ref, vt_ref, o_ref, m_scr, acc_scr, *, tq):
    p = pl.program_id(1)
    i = qi_ref[p]
    j = kj_ref[p]

    @pl.when(j == 0)
    def _():
        m_scr[...] = jnp.full_like(m_scr, -jnp.inf)
        acc_scr[...] = jnp.zeros_like(acc_scr)

    def update(masked):
        s = _dot(k_ref[...], qt_ref[...])
        if masked:
            kpos = lax.broadcasted_iota(jnp.int32, s.shape, 0)
            qpos = lax.broadcasted_iota(jnp.int32, s.shape, 1) & (tq - 1)
            s = jnp.where(kpos <= qpos, s, -jnp.inf)
        m_prev = m_scr[...]
        m_new = jnp.maximum(m_prev, jnp.max(s, axis=0, keepdims=True))
        alpha = jnp.exp(m_prev - m_new)
        pr = jnp.exp(s - m_new).astype(BF16)
        acc_scr[...] = alpha * acc_scr[...] + _dot(vt_ref[...], pr)
        m_scr[...] = m_new

    @pl.when(j < i)
    def _():
        update(False)

    @pl.when(j == i)
    def _():
        update(True)
        out = acc_scr[0:KV_RANK, :] / acc_scr[KV_RANK:KV_RANK + 1, :]
        for hd in range(MLA_HEADS):
            o_ref[:, hd * KV_RANK:(hd + 1) * KV_RANK] = out[:, hd * tq:(hd + 1) * tq].T.astype(o_ref.dtype)


def _attend_prompt(qt, kcat, ckvt, batch, seq, tq):
    nq = seq // tq
    pairs = [(i, j) for i in range(nq) for j in range(i + 1)]
    qi = jnp.asarray(np.array([p[0] for p in pairs], np.int32))
    kj = jnp.asarray(np.array([p[1] for p in pairs], np.int32))
    cols = MLA_HEADS * tq
    return pl.pallas_call(
        functools.partial(_attn_body, tq=tq),
        out_shape=jax.ShapeDtypeStruct((batch, seq, MLA_HEADS * KV_RANK), BF16),
        grid_spec=pltpu.PrefetchScalarGridSpec(
            num_scalar_prefetch=2,
            grid=(batch, len(pairs)),
            in_specs=[pl.BlockSpec((None, None, QK_LAT, cols), lambda b, p, qi, kj: (b, qi[p], 0, 0)),
                      pl.BlockSpec((None, tq, QK_LAT), lambda b, p, qi, kj: (b, kj[p], 0)),
                      pl.BlockSpec((None, VT_ROWS, tq), lambda b, p, qi, kj: (b, 0, kj[p]))],
            out_specs=pl.BlockSpec((None, tq, MLA_HEADS * KV_RANK), lambda b, p, qi, kj: (b, qi[p], 0)),
            scratch_shapes=[pltpu.VMEM((1, cols), F32), pltpu.VMEM((VT_ROWS, cols), F32)],
        ),
        compiler_params=_params("parallel", "arbitrary"),
        name="attend_prompt",
    )(qi, kj, qt, kcat.reshape(batch, seq, QK_LAT), ckvt)


PAGED_CHUNK_PAGES = 32
PAGED_SLOTS = 4


def _paged_body(pt_ref, q_ref, cn_ref, kn_ref, ckv_hbm, krt_hbm, o_ref, cbuf, kbuf, sems, *, n_pages):
    ch = PAGED_CHUNK_PAGES
    nc = n_pages // ch
    ahead = PAGED_SLOTS - 1
    nseq = pl.num_programs(0)
    b = pl.program_id(0)

    def copies(g, slot):
        out = []
        for pg in range(ch):
            page = pt_ref[g * ch + pg]
            dst = pl.ds(pg * PAGE_SIZE, PAGE_SIZE)
            out.append(pltpu.make_async_copy(ckv_hbm.at[page], cbuf.at[slot, dst, :], sems.at[0, slot]))
            out.append(pltpu.make_async_copy(krt_hbm.at[page], kbuf.at[slot, pg], sems.at[1, slot]))
        return out

    def start(g, slot):
        for cp in copies(g, slot):
            cp.start()

    def wait(slot):
        for cp in copies(0, slot):
            cp.wait()

    @pl.when(b == 0)
    def _():
        for c in range(ahead):
            start(c, c)

    q = q_ref[...]
    ql = q[:, 0:KV_RANK]
    qr = q[:, KV_RANK:QK_LAT]
    cn = cn_ref[...]
    kn = kn_ref[...]
    m = jnp.sum(ql * cn, axis=1, keepdims=True) + jnp.sum(qr * kn, axis=1, keepdims=True)
    l = jnp.ones_like(m)
    acc = jnp.broadcast_to(cn, (MLA_HEADS, KV_RANK))
    qlb = ql.astype(BF16)
    qrb = qr.astype(BF16)
    for c in range(nc):
        slot = c % PAGED_SLOTS
        g = b * nc + c
        nxt = (c + ahead) % PAGED_SLOTS
        if c + ahead < nc:
            start(g + ahead, nxt)
        else:
            @pl.when(b + 1 < nseq)
            def _():
                start(g + ahead, nxt)
        wait(slot)
        kc = cbuf[slot].astype(BF16)
        s_rope = [_dot(qrb, kbuf[slot, pg].astype(BF16)) for pg in range(ch)]
        s = _dot_nt(qlb, kc) + jnp.concatenate(s_rope, axis=1)
        m_new = jnp.maximum(m, jnp.max(s, axis=1, keepdims=True))
        alpha = jnp.exp(m - m_new)
        pr = jnp.exp(s - m_new)
        l = alpha * l + jnp.sum(pr, axis=1, keepdims=True)
        acc = alpha * acc + _dot(pr.astype(BF16), kc)
        m = m_new
    o_ref[...] = acc / l


def _attend_paged(q, ckv_new, krope_new, cache_ckv, cache_krope_t, page_table):
    nseq, n_pages = page_table.shape
    ch = PAGED_CHUNK_PAGES
    assert n_pages % (PAGED_SLOTS * ch) == 0
    per_seq = lambda *tail: pl.BlockSpec((None,) + tail, lambda i, pt: (i,) + (0,) * len(tail))
    return pl.pallas_call(
        functools.partial(_paged_body, n_pages=n_pages),
        out_shape=jax.ShapeDtypeStruct((nseq, MLA_HEADS, KV_RANK), F32),
        grid_spec=pltpu.PrefetchScalarGridSpec(
            num_scalar_prefetch=1,
            grid=(nseq,),
            in_specs=[per_seq(MLA_HEADS, QK_LAT), per_seq(1, KV_RANK), per_seq(1, QK_ROPE),
                      pl.BlockSpec(memory_space=pl.ANY), pl.BlockSpec(memory_space=pl.ANY)],
            out_specs=per_seq(MLA_HEADS, KV_RANK),
            scratch_shapes=[pltpu.VMEM((PAGED_SLOTS, ch * PAGE_SIZE, KV_RANK), F32),
                            pltpu.VMEM((PAGED_SLOTS, ch, QK_ROPE, PAGE_SIZE), F32),
                            pltpu.SemaphoreType.DMA((2, PAGED_SLOTS))],
        ),
        compiler_params=_params("arbitrary"),
        name="attend_paged",
    )(page_table.reshape(-1), q, ckv_new.reshape(nseq, 1, KV_RANK), krope_new.reshape(nseq, 1, QK_ROPE),
      cache_ckv, cache_krope_t)


def _attn_out_body(o_ref, wuv_ref, wo_ref, x_ref, gate_ref, out_ref):
    parts = [_dot(o_ref[:, hd * KV_RANK:(hd + 1) * KV_RANK], wuv_ref[hd]).astype(BF16)
             for hd in range(MLA_HEADS)]
    o = jnp.concatenate(parts, axis=1)
    out_ref[...] = x_ref[...] + gate_ref[...] * _dot(o, wo_ref[...])


def _attn_out(grp, o_lat, w_uv_t, w_out, layer_b, x, mods, sub):
    width = MLA_HEADS * KV_RANK
    return pl.pallas_call(
        _attn_out_body,
        out_shape=jax.ShapeDtypeStruct((grp.rows, D_MODEL), F32),
        grid=(grp.rows // grp.tm,),
        in_specs=[_row_spec(grp, width),
                  _full_spec(w_uv_t.shape),
                  pl.BlockSpec((None, MLA_HEADS * V_HEAD, D_MODEL), lambda i: (layer_b, 0, 0)),
                  _row_spec(grp, D_MODEL),
                  _mod_spec(grp, mods, sub * 3 + 2)],
        out_specs=_row_spec(grp, D_MODEL),
        compiler_params=_params("parallel"),
        name="attn_out",
    )(o_lat, w_uv_t, w_out, x, mods)


def _rope_tables(pos):
    half = QK_ROPE // 2
    inv_freq = ROPE_THETA ** (-2.0 * jnp.arange(half, dtype=F32) / QK_ROPE)
    ang = pos.astype(F32)[:, None] * inv_freq[None, :]
    cos, sin = jnp.cos(ang), jnp.sin(ang)
    return jnp.concatenate([cos, cos], axis=-1), jnp.concatenate([-sin, sin], axis=-1)


def _swap_halves(w):
    half = QK_ROPE // 2
    return jnp.concatenate([w[..., half:], w[..., :half]], axis=-1)


def kernel(x_prompt, x_sample, c_prompt, c_sample, state_hgrn, cache_ckv, cache_krope, page_table, w_ada, b_ada, g_norm, w_ffn_gu, w_ffn_down, hg_w_in, hg_lb, hg_g_out, hg_w_out, kv_w_ada, kv_b_ada, kv_g_norm, kv_w_down, kv_g_latent, kv_w_uk, kv_w_uv, q_w_down, q_g_norm, q_w_up, attn_w_out, g_final):
    batch, seq, d = x_prompt.shape
    nseq = x_sample.shape[0]
    n_b = DEPTH - N_A_LAYERS

    w_in = hg_w_in.astype(BF16)
    w_ho = hg_w_out.astype(BF16)
    w_ao = attn_w_out.astype(BF16)
    w_kc = kv_w_down[:, :KV_RANK].astype(BF16)
    w_kr = kv_w_down[:, KV_RANK:].astype(BF16)
    w_krr = _swap_halves(kv_w_down[:, KV_RANK:]).astype(BF16)
    w_dq = q_w_down.astype(BF16)
    w_qn = q_w_up[..., :QK_NOPE].reshape(n_b, Q_RANK, MLA_HEADS * QK_NOPE).astype(BF16)
    w_qr = q_w_up[..., QK_NOPE:].reshape(n_b, Q_RANK, MLA_HEADS * QK_ROPE)
    w_qrr = _swap_halves(q_w_up[..., QK_NOPE:]).reshape(n_b, Q_RANK, MLA_HEADS * QK_ROPE)
    w_qrt = w_qr.transpose(0, 2, 1).astype(BF16)
    w_qrrt = w_qrr.transpose(0, 2, 1).astype(BF16)
    w_ukt = kv_w_uk.transpose(1, 0, 2).astype(BF16)
    w_uv_t = kv_w_uv.transpose(1, 0, 2).astype(BF16)
    cache_krope_t = cache_krope.transpose(0, 2, 1)
    g_q_rows = q_g_norm.reshape(n_b, 1, Q_RANK)
    g_out_rows = hg_g_out.reshape(N_A_LAYERS, 1, D_MODEL)

    pad = 8
    c_all = jnp.concatenate([c_prompt, jnp.zeros((pad - batch, d), F32), c_sample], axis=0)
    mods_all = [_ada(c_all, w_ada, b_ada.reshape(DEPTH, 1, -1), l) for l in range(DEPTH)]
    kvm_all = _ada(c_all, kv_w_ada[None], kv_b_ada.reshape(1, 1, -1), 0)

    def trunk(grp, x, mods_l, kvmods, pos, mix, attend, stream_dtype):
        grp_half = grp._replace(tm=min(grp.tm, WIDE_TM))
        grp_q = grp._replace(tm=min(grp.tm, ATTN_TQ))
        cos_k, sin_k = _rope_tables(pos)
        cos_qt, sin_qt = jnp.tile(cos_k, (1, MLA_HEADS)).T, jnp.tile(sin_k, (1, MLA_HEADS)).T
        states = []
        ckv = krope = kcat = ckvt = None
        for l in range(DEPTH):
            mods = mods_l[l]
            gn = lambda j: g_norm[l, j].reshape(1, d)
            if l == N_A_LAYERS:
                ckv, krope, kcat, ckvt = _shared_kv(grp, x, kvmods, kv_g_norm.reshape(1, d), w_kc, w_kr, w_krr,
                                                    kv_g_latent.reshape(1, KV_RANK), cos_k, sin_k)
            x = _ffn(grp, x, mods, 0, gn(0), w_ffn_gu, w_ffn_down, l, 0)
            if l < N_A_LAYERS:
                qs, k, v, og, lg = _hgrn_in(grp_half, x, mods, gn(1), w_in, hg_lb, l, stream_dtype)
                y, s = mix(l, qs, k, v, og, lg)
                states.append(s)
                x = _proj_res(grp, y, w_ho, l, x, mods, 1)
            else:
                lb_ = l - N_A_LAYERS
                qt = _mla_q(grp_q, x, mods, gn(1), w_dq, g_q_rows, w_qn, w_qrt, w_qrrt, w_ukt, cos_qt, sin_qt,
                            lb_)
                o_lat = attend(qt, ckv, krope, kcat, ckvt)
                x = _attn_out(grp, o_lat, w_uv_t, w_ao, lb_, x, mods, 1)
            x = _ffn(grp, x, mods, 2, gn(2), w_ffn_gu, w_ffn_down, l, 1,
                     g_final=g_final.reshape(1, d) if l == DEPTH - 1 else None)
        return x, jnp.stack(states), ckv, krope

    grp_p = Group(rows=batch * seq, seq=seq, tm=PROMPT_TM, per_row=False)
    mods_p = [m[:pad].reshape(pad, 1, -1) for m in mods_all]
    kvm_p = kvm_all[:pad].reshape(pad, 1, -1)

    def mix_p(l, qs, k, v, og, lg):
        y, s = _gla_prompt(qs, k, v, og, lg, g_out_rows[l], batch, seq)
        return y.reshape(batch * seq, d), s

    def attend_p(qt, ckv, krope, kcat, ckvt):
        o = _attend_prompt(qt, kcat, ckvt, batch, seq, ATTN_TQ)
        return o.reshape(batch * seq, MLA_HEADS * KV_RANK)

    y_p, st_p, ckv_p, kr_p = trunk(grp_p, x_prompt.reshape(batch * seq, d), mods_p, kvm_p,
                                   jnp.arange(seq), mix_p, attend_p, BF16)

    assert nseq == MIN_TM
    grp_s = Group(rows=nseq, seq=nseq, tm=nseq, per_row=True)
    mods_s = [m[pad:] for m in mods_all]
    kvm_s = kvm_all[pad:]

    def mix_s(l, qs, k, v, og, lg):
        y, s = _gla_step(qs, k, v, og, lg, hg_g_out[l].reshape(HG_HEADS, HG_DV), state_hgrn, l)
        return y.reshape(nseq, d), s

    def attend_s(qt, ckv, krope, kcat, ckvt):
        qf = qt.reshape(QK_LAT, MLA_HEADS, nseq).transpose(2, 1, 0).astype(F32)
        o = _attend_paged(qf, ckv, krope, cache_ckv, cache_krope_t, page_table)
        return o.reshape(nseq, MLA_HEADS * KV_RANK).astype(BF16)

    y_s, st_s, ckv_s, kr_s = trunk(grp_s, x_sample.reshape(nseq, d), mods_s, kvm_s,
                                   jnp.full((nseq,), PAST_LEN), mix_s, attend_s, F32)

    return (y_p.reshape(batch, seq, d), y_s.reshape(nseq, 1, d),
            st_p.astype(state_hgrn.dtype), st_s.astype(state_hgrn.dtype),
            ckv_p.reshape(batch, seq, KV_RANK), kr_p.reshape(batch, seq, QK_ROPE),
            ckv_s.reshape(nseq, 1, KV_RANK), kr_s.reshape(nseq, 1, QK_ROPE))
```

```python
import functools
from typing import NamedTuple

import jax
import jax.numpy as jnp
import numpy as np
from jax import lax
from jax.experimental import pallas as pl
from jax.experimental.pallas import tpu as pltpu

D_MODEL = 1024
DEPTH = 2
N_A_LAYERS = DEPTH // 2
PAST_LEN = 16384
PAGE_SIZE = 128
D_FF = 2816
HG_HEADS = 8
HG_DK = D_MODEL // HG_HEADS
HG_DV = D_MODEL // HG_HEADS
MLA_HEADS = 8
QK_NOPE = 128
QK_ROPE = 64
V_HEAD = 128
KV_RANK = 256
Q_RANK = 384
QK_LAT = KV_RANK + QK_ROPE
ROPE_THETA = 10000.0
EPS = 1e-6
SM_SCALE = (QK_NOPE + QK_ROPE) ** -0.5

BF16 = jnp.bfloat16
F32 = jnp.float32

V7X_VMEM_BYTES = 64 * 1024 * 1024
VMEM_LIMIT = V7X_VMEM_BYTES * 3 // 4

PROMPT_TM = 1024
WIDE_TM = 512
MIN_TM = 128
ADA_TN = 1024
STEP_SEQS = 8

GLA_CHUNK = 64
GLA_SUB = 16
GLA_BLOCK = 512
GLA_SAFE_DECAY = 60.0

ATTN_TQ = 512
VT_ROWS = KV_RANK + 16

NT_DIMS = (((1,), (1,)), ((), ()))


class Group(NamedTuple):
    rows: int
    seq: int
    tm: int
    per_row: bool

    @property
    def tiles_per_seq(self):
        return self.seq // self.tm


def _params(*sem):
    return pltpu.CompilerParams(dimension_semantics=sem, vmem_limit_bytes=VMEM_LIMIT)


def _dot(a, b):
    return jnp.dot(a, b, preferred_element_type=F32)


def _dot_nt(a, b):
    return lax.dot_general(a, b, NT_DIMS, preferred_element_type=F32)


def _sigmoid(x):
    return 1.0 / (1.0 + jnp.exp(-x))


def _rms(x):
    return x * lax.rsqrt(jnp.mean(x * x, axis=-1, keepdims=True) + EPS)


def _mod_norm(x, g, scale, shift):
    return (_rms(x) * g) * (1.0 + scale) + shift


def _mod_spec(grp, mods, col):
    if grp.per_row:
        return pl.BlockSpec((grp.tm, D_MODEL), lambda *g: (g[0], col))
    tps = grp.tiles_per_seq
    return pl.BlockSpec((None, 1, D_MODEL), lambda *g: (g[0] // tps, 0, col))


def _pos_spec(grp, width):
    if grp.per_row:
        return pl.BlockSpec((grp.tm, width), lambda *g: (g[0], 0))
    tps = grp.tiles_per_seq
    return pl.BlockSpec((grp.tm, width), lambda *g: (g[0] % tps, 0))


def _row_spec(grp, width):
    return pl.BlockSpec((grp.tm, width), lambda *g: (g[0], 0))


def _full_spec(shape):
    nd = len(shape)
    return pl.BlockSpec(shape, lambda *g: (0,) * nd)


def _ada_body(c_ref, w_ref, b_ref, o_ref):
    c = c_ref[...]
    sc = (c * _sigmoid(c)).astype(BF16)
    o_ref[...] = _dot(sc, w_ref[...].astype(BF16)) + b_ref[...]


def _ada(c_all, w, b, layer, tn=ADA_TN):
    rows = c_all.shape[0]
    n = w.shape[-1]
    return pl.pallas_call(
        _ada_body,
        out_shape=jax.ShapeDtypeStruct((rows, n), F32),
        grid=(n // tn,),
        in_specs=[
            _full_spec((rows, D_MODEL)),
            pl.BlockSpec((None, D_MODEL, tn), lambda j: (layer, 0, j)),
            pl.BlockSpec((None, 1, tn), lambda j: (layer, 0, j)),
        ],
        out_specs=pl.BlockSpec((rows, tn), lambda j: (0, j)),
        compiler_params=_params("arbitrary"),
        name="ada_proj",
    )(c_all, w, b)


FFN_TF = 256
FFN_GROUP = 3
FFN_ROW_PARTS = 2


def _ffn_body(x_ref, shift_ref, scale_ref, gate_ref, g_ref, wgu_hbm, wd_hbm, *rest, layer, half, final_norm):
    if final_norm:
        gf_ref, o_ref, h_scr, acc_scr, wg_buf, wu_buf, wd_buf, sems = rest
    else:
        o_ref, h_scr, acc_scr, wg_buf, wu_buf, wd_buf, sems = rest
    tf = FFN_TF
    nf = D_FF // tf
    groups = [list(range(s, min(s + FFN_GROUP, nf))) for s in range(0, nf, FFN_GROUP)]
    ng = len(groups)
    assert ng % 2 == 0
    i = pl.program_id(0)
    ni = pl.num_programs(0)

    def copies(g, slot):
        out = []
        for t, j in enumerate(groups[g]):
            out += [
                pltpu.make_async_copy(wgu_hbm.at[layer, half, :, pl.ds(j * tf, tf)], wg_buf.at[slot, t],
                                      sems.at[0, slot]),
                pltpu.make_async_copy(wgu_hbm.at[layer, half, :, pl.ds((nf + j) * tf, tf)], wu_buf.at[slot, t],
                                      sems.at[1, slot]),
                pltpu.make_async_copy(wd_hbm.at[layer, half, pl.ds(j * tf, tf), :], wd_buf.at[slot, t],
                                      sems.at[2, slot]),
            ]
        return out

    def start(g, slot):
        for cp in copies(g, slot):
            cp.start()

    def wait(g, slot):
        for cp in copies(g, slot):
            cp.wait()

    @pl.when(i == 0)
    def _():
        start(0, 0)

    tm = x_ref.shape[0]
    n_parts = FFN_ROW_PARTS if tm % (FFN_ROW_PARTS * 16) == 0 and tm > MIN_TM else 1
    parts = [slice(p * tm // n_parts, (p + 1) * tm // n_parts) for p in range(n_parts)]
    rows_of = lambda ref, p: ref[p, :] if ref.shape[0] == tm else ref[...]

    for g in range(ng):
        slot = g % 2
        if g + 1 < ng:
            start(g + 1, 1 - slot)
        else:
            @pl.when(i + 1 < ni)
            def _():
                start(0, 1 - slot)
        wait(g, slot)
        for t, j in enumerate(groups[g]):
            wg = wg_buf[slot, t].astype(BF16)
            wu = wu_buf[slot, t].astype(BF16)
            wd = wd_buf[slot, t].astype(BF16)
            for p in parts:
                if j == 0:
                    h_scr[p, :] = _mod_norm(x_ref[p, :], g_ref[...], rows_of(scale_ref, p),
                                            rows_of(shift_ref, p)).astype(BF16)
                h = h_scr[p, :]
                a = _dot(h, wg)
                b = _dot(h, wu)
                act = ((a * _sigmoid(a)) * b).astype(BF16)
                part = _dot(act, wd)
                if j == 0:
                    acc_scr[p, :] = part
                else:
                    acc_scr[p, :] += part

    y = x_ref[...] + (0.5 * gate_ref[...]) * acc_scr[...]
    if final_norm:
        y = _rms(y) * gf_ref[...]
    o_ref[...] = y


def _ffn(grp, x, mods, sub, g_norm_row, w_gu, w_down, layer, half, g_final=None):
    tf = FFN_TF
    final_norm = g_final is not None
    in_specs = [
        _row_spec(grp, D_MODEL),
        _mod_spec(grp, mods, sub * 3 + 0),
        _mod_spec(grp, mods, sub * 3 + 1),
        _mod_spec(grp, mods, sub * 3 + 2),
        _full_spec((1, D_MODEL)),
        pl.BlockSpec(memory_space=pl.ANY),
        pl.BlockSpec(memory_space=pl.ANY),
    ]
    args = [x, mods, mods, mods, g_norm_row, w_gu, w_down]
    if final_norm:
        in_specs.append(_full_spec((1, D_MODEL)))
        args.append(g_final)
    return pl.pallas_call(
        functools.partial(_ffn_body, layer=layer, half=half, final_norm=final_norm),
        out_shape=jax.ShapeDtypeStruct((grp.rows, D_MODEL), F32),
        grid=(grp.rows // grp.tm,),
        in_specs=in_specs,
        out_specs=_row_spec(grp, D_MODEL),
        scratch_shapes=[pltpu.VMEM((grp.tm, D_MODEL), BF16), pltpu.VMEM((grp.tm, D_MODEL), F32),
                        pltpu.VMEM((2, FFN_GROUP, D_MODEL, tf), F32), pltpu.VMEM((2, FFN_GROUP, D_MODEL, tf), F32),
                        pltpu.VMEM((2, FFN_GROUP, tf, D_MODEL), F32), pltpu.SemaphoreType.DMA((3, 2))],
        compiler_params=_params("arbitrary"),
        name="ffn_half",
    )(*args)


def _hgrn_in_body(x_ref, shift_ref, scale_ref, g_ref, w_ref, lb_ref, qs_ref, k_ref, v_ref, og_ref, lg_ref,
                  *, layer):
    d = D_MODEL
    h = _mod_norm(x_ref[...], g_ref[...], scale_ref[...], shift_ref[...]).astype(BF16)
    lbp = lb_ref[...]
    e = jnp.exp(lbp - jnp.max(lbp, axis=0, keepdims=True))
    lb = jnp.sum(e[: layer + 1], axis=0, keepdims=True) / jnp.sum(e, axis=0, keepdims=True)

    f = _dot(h, w_ref[:, d:2 * d])
    q = _dot(h, w_ref[:, 0:d])
    g = _dot(h, w_ref[:, 3 * d:4 * d])
    v_ref[...] = _dot(h, w_ref[:, 2 * d:3 * d]).astype(v_ref.dtype)
    u = jnp.exp(-jnp.abs(f))
    r = 1.0 / (1.0 + u)
    ur = u * r
    pos = f >= 0.0
    lg_ref[...] = jnp.log(lb + (1.0 - lb) * jnp.where(pos, r, ur))
    k_ref[...] = ((1.0 - lb) * jnp.where(pos, ur, r)).astype(k_ref.dtype)
    qs_ref[...] = (q * _sigmoid(q)).astype(qs_ref.dtype)
    og_ref[...] = (g * _sigmoid(g)).astype(og_ref.dtype)


def _hgrn_in(grp, x, mods, g_norm_row, w_in, hg_lb, layer, stream_dtype):
    sub = 1
    outs = ([jax.ShapeDtypeStruct((grp.rows, D_MODEL), stream_dtype)] * 4
            + [jax.ShapeDtypeStruct((grp.rows, D_MODEL), F32)])
    return pl.pallas_call(
        functools.partial(_hgrn_in_body, layer=layer),
        out_shape=outs,
        grid=(grp.rows // grp.tm,),
        in_specs=[
            _row_spec(grp, D_MODEL),
            _mod_spec(grp, mods, sub * 3 + 0),
            _mod_spec(grp, mods, sub * 3 + 1),
            _full_spec((1, D_MODEL)),
            pl.BlockSpec((None, D_MODEL, 4 * D_MODEL), lambda i: (layer, 0, 0)),
            _full_spec(hg_lb.shape),
        ],
        out_specs=[_row_spec(grp, D_MODEL)] * 5,
        compiler_params=_params("parallel"),
        name="hgrn_in",
    )(x, mods, mods, g_norm_row, w_in, hg_lb)


def _segmented_cumsum(x, seg):
    row = lax.broadcasted_iota(jnp.int32, x.shape, 0) & (seg - 1)
    s = 1
    while s < seg:
        x = x + jnp.where(row >= s, pltpu.roll(x, s, 0), 0.0)
        s *= 2
    return x


def _gla_body(q_ref, k_ref, v_ref, og_ref, lg_ref, go_ref, y_ref, s_out_ref, st_scr, b_scr):
    c, sub = GLA_CHUNK, GLA_SUB
    t = pl.program_id(1)

    @pl.when(t == 0)
    def _():
        st_scr[...] = jnp.zeros_like(st_scr)

    b_scr[...] = _segmented_cumsum(lg_ref[...], c)
    tri = lax.broadcasted_iota(jnp.int32, (c, c), 1) <= lax.broadcasted_iota(jnp.int32, (c, c), 0)

    def finish(r0, h, o):
        cols = slice(h * HG_DV, (h + 1) * HG_DV)
        y = _rms(o) * go_ref[:, cols] * og_ref[pl.ds(r0, c), cols].astype(F32)
        y_ref[pl.ds(r0, c), cols] = y.astype(y_ref.dtype)

    n_chunks = q_ref.shape[0] // c

    def chunk_refs(r0):
        b_all = b_scr[pl.ds(r0, c), :]
        return b_all, [jnp.zeros((1, D_MODEL), F32)] + [b_all[i * sub - 1:i * sub, :] for i in range(1, c // sub)]

    worst = jnp.zeros((1, D_MODEL), F32)
    for ci in range(n_chunks):
        b_all, ref_rows = chunk_refs(ci * c)
        for i in range(c // sub):
            worst = jnp.maximum(worst, ref_rows[i] - b_all[(i + 1) * sub - 1:(i + 1) * sub, :])
    safe = jnp.max(worst) <= GLA_SAFE_DECAY

    @pl.when(safe)
    def _():
        for ci in range(n_chunks):
            r0 = ci * c
            b_all, ref_rows = chunk_refs(r0)
            stage = []
            for h in range(HG_HEADS):
                cols = slice(h * HG_DK, (h + 1) * HG_DK)
                b = b_all[:, cols]
                b_last = b[c - 1:c, :]
                refs = [r[:, cols] for r in ref_rows]
                q = q_ref[pl.ds(r0, c), cols].astype(F32)
                k = k_ref[pl.ds(r0, c), cols].astype(F32)
                v = v_ref[pl.ds(r0, c), cols]
                st = st_scr[h]
                o_inter = _dot_nt((q * jnp.exp(b)).astype(BF16), st.astype(BF16))
                rows = []
                for i in range(c // sub):
                    qi = (q[i * sub:(i + 1) * sub] * jnp.exp(b[i * sub:(i + 1) * sub] - refs[i])).astype(BF16)
                    ki = (k * jnp.exp(jnp.minimum(refs[i] - b, GLA_SAFE_DECAY))).astype(BF16)
                    rows.append(_dot_nt(qi, ki))
                kd = (k * jnp.exp(b_last - b)).astype(BF16)
                vt = v.astype(F32).T.astype(BF16)
                st_scr[h] = st * jnp.exp(b_last) + _dot(vt, kd)
                stage.append((o_inter, rows, v))
            for h, (o_inter, rows, v) in enumerate(stage):
                att = jnp.where(tri, jnp.concatenate(rows, axis=0), 0.0)
                finish(r0, h, o_inter + _dot(att.astype(BF16), v))

    @pl.when(jnp.logical_not(safe))
    def _():
        lane = lax.broadcasted_iota(jnp.int32, (HG_DV, c), 1)
        sublane = lax.broadcasted_iota(jnp.int32, (c, HG_DK), 0)

        def slow_chunk(ci, carry):
            r0 = pl.multiple_of(ci * c, c)
            for h in range(HG_HEADS):
                cols = slice(h * HG_DK, (h + 1) * HG_DK)
                vt = v_ref[pl.ds(r0, c), cols].astype(F32).T
                qf = q_ref[pl.ds(r0, c), cols].astype(F32)
                kf = k_ref[pl.ds(r0, c), cols].astype(F32)
                df = jnp.exp(lg_ref[pl.ds(r0, c), cols])

                def token(ti, carry2, vt=vt, qf=qf, kf=kf, df=df):
                    st, ot = carry2
                    sel = lane == ti
                    pick = lambda a: jnp.sum(jnp.where(sublane == ti, a, 0.0), axis=0, keepdims=True)
                    v_col = jnp.sum(jnp.where(sel, vt, 0.0), axis=1, keepdims=True)
                    st = st * pick(df) + v_col * pick(kf)
                    o_col = jnp.sum(st * pick(qf), axis=1, keepdims=True)
                    return st, jnp.where(sel, o_col, ot)

                st, ot = lax.fori_loop(0, c, token, (st_scr[h], jnp.zeros((HG_DV, c), F32)))
                st_scr[h] = st
                finish(r0, h, ot.T)
            return carry

        lax.fori_loop(0, n_chunks, slow_chunk, 0)

    @pl.when(t == pl.num_programs(1) - 1)
    def _():
        for h in range(HG_HEADS):
            s_out_ref[h] = st_scr[h].T


def _gla_prompt(qs, k, v, og, lg, g_out_row, batch, seq):
    tb = GLA_BLOCK
    stream = pl.BlockSpec((None, tb, D_MODEL), lambda b, t: (b, t, 0))
    shp = (batch, seq, D_MODEL)
    return pl.pallas_call(
        _gla_body,
        out_shape=[jax.ShapeDtypeStruct(shp, BF16),
                   jax.ShapeDtypeStruct((batch, HG_HEADS, HG_DK, HG_DV), F32)],
        grid=(batch, seq // tb),
        in_specs=[stream, stream, stream, stream, stream, _full_spec((1, D_MODEL))],
        out_specs=[stream, pl.BlockSpec((None, HG_HEADS, HG_DK, HG_DV), lambda b, t: (b, 0, 0, 0))],
        scratch_shapes=[pltpu.VMEM((HG_HEADS, HG_DV, HG_DK), F32), pltpu.VMEM((tb, D_MODEL), F32)],
        compiler_params=_params("parallel", "arbitrary"),
        name="gla_chunked",
    )(qs.reshape(shp), k.reshape(shp), v.reshape(shp), og.reshape(shp), lg.reshape(shp), g_out_row)


def _gla_step_body(q_ref, k_ref, v_ref, og_ref, lg_ref, go_ref, s_ref, y_ref, s_out_ref):
    go = go_ref[...]

    def one(bi, carry):
        qt = q_ref[bi].astype(F32).T
        kt = k_ref[bi].astype(F32).T
        dt = jnp.exp(lg_ref[bi]).T
        v = v_ref[bi].astype(F32)
        outs = []
        for h in range(HG_HEADS):
            s = s_ref[bi, h] * dt[:, h:h + 1] + kt[:, h:h + 1] * v[h:h + 1, :]
            s_out_ref[bi, h] = s
            outs.append(jnp.sum(s * qt[:, h:h + 1], axis=0, keepdims=True))
        o = jnp.concatenate(outs, axis=0)
        y_ref[bi] = (_rms(o) * go * og_ref[bi].astype(F32)).astype(y_ref.dtype)
        return carry

    lax.fori_loop(0, q_ref.shape[0], one, 0)


def _gla_step(qs, k, v, og, lg, g_out_heads, state, layer, bb=STEP_SEQS):
    rows = qs.shape[0]
    shp = (rows, HG_HEADS, HG_DK)
    stream = pl.BlockSpec((bb, HG_HEADS, HG_DK), lambda i: (i, 0, 0))
    return pl.pallas_call(
        _gla_step_body,
        out_shape=[jax.ShapeDtypeStruct(shp, F32),
                   jax.ShapeDtypeStruct((rows, HG_HEADS, HG_DK, HG_DV), F32)],
        grid=(rows // bb,),
        in_specs=[stream, stream, stream, stream, stream, _full_spec((HG_HEADS, HG_DV)),
                  pl.BlockSpec((None, bb, HG_HEADS, HG_DK, HG_DV), lambda i: (layer, i, 0, 0, 0))],
        out_specs=[stream, pl.BlockSpec((bb, HG_HEADS, HG_DK, HG_DV), lambda i: (i, 0, 0, 0))],
        compiler_params=_params("parallel"),
        name="gla_step",
    )(qs.reshape(shp), k.reshape(shp), v.reshape(shp), og.reshape(shp), lg.reshape(shp), g_out_heads, state)


def _proj_res_body(a_ref, w_ref, x_ref, gate_ref, o_ref):
    o_ref[...] = x_ref[...] + gate_ref[...] * _dot(a_ref[...].astype(BF16), w_ref[...])


def _proj_res(grp, a, w, layer, x, mods, sub):
    kdim = a.shape[1]
    return pl.pallas_call(
        _proj_res_body,
        out_shape=jax.ShapeDtypeStruct((grp.rows, D_MODEL), F32),
        grid=(grp.rows // grp.tm,),
        in_specs=[_row_spec(grp, kdim),
                  pl.BlockSpec((None, kdim, D_MODEL), lambda i: (layer, 0, 0)),
                  _row_spec(grp, D_MODEL),
                  _mod_spec(grp, mods, sub * 3 + 2)],
        out_specs=_row_spec(grp, D_MODEL),
        compiler_params=_params("parallel"),
        name="proj_residual",
    )(a, w, x, mods)


def _kv_body(x_ref, shift_ref, scale_ref, g_ref, wc_ref, wr_ref, wrr_ref, gl_ref, cos_ref, sin_ref,
             ckv_ref, kr_ref, kcat_ref, ckvt_ref):
    h = _mod_norm(x_ref[...], g_ref[...], scale_ref[...], shift_ref[...]).astype(BF16)
    ckv = _rms(_dot(h, wc_ref[...])) * gl_ref[...]
    kr = _dot(h, wr_ref[...]) * cos_ref[...] + _dot(h, wrr_ref[...]) * sin_ref[...]
    ckv_ref[...] = ckv
    kr_ref[...] = kr
    kcat_ref[:, 0:KV_RANK] = ckv.astype(BF16)
    kcat_ref[:, KV_RANK:QK_LAT] = kr.astype(BF16)
    ckvt_ref[0:KV_RANK, :] = ckv.T.astype(BF16)
    ckvt_ref[KV_RANK:VT_ROWS, :] = jnp.ones((VT_ROWS - KV_RANK, ckvt_ref.shape[1]), BF16)


def _shared_kv(grp, x, kvmods, g_norm_row, w_c, w_r, w_rr, g_latent_row, cos_t, sin_t):
    tps = grp.tiles_per_seq
    return pl.pallas_call(
        _kv_body,
        out_shape=[jax.ShapeDtypeStruct((grp.rows, KV_RANK), F32),
                   jax.ShapeDtypeStruct((grp.rows, QK_ROPE), F32),
                   jax.ShapeDtypeStruct((grp.rows, QK_LAT), BF16),
                   jax.ShapeDtypeStruct((grp.rows // grp.seq, VT_ROWS, grp.seq), BF16)],
        grid=(grp.rows // grp.tm,),
        in_specs=[_row_spec(grp, D_MODEL),
                  _mod_spec(grp, kvmods, 0),
                  _mod_spec(grp, kvmods, 1),
                  _full_spec((1, D_MODEL)),
                  _full_spec(w_c.shape), _full_spec(w_r.shape), _full_spec(w_rr.shape),
                  _full_spec((1, KV_RANK)),
                  _pos_spec(grp, QK_ROPE), _pos_spec(grp, QK_ROPE)],
        out_specs=[_row_spec(grp, KV_RANK), _row_spec(grp, QK_ROPE), _row_spec(grp, QK_LAT),
                   pl.BlockSpec((None, VT_ROWS, grp.tm), lambda i: (i // tps, 0, i % tps))],
        compiler_params=_params("parallel"),
        name="shared_kv",
    )(x, kvmods, kvmods, g_norm_row, w_c, w_r, w_rr, g_latent_row, cos_t, sin_t)


def _mla_q_body(x_ref, shift_ref, scale_ref, g_ref, wdq_ref, gq_ref, wn_ref, wrt_ref, wrrt_ref, wukt_ref,
                cos_ref, sin_ref, q_ref):
    tm = x_ref.shape[0]
    h = _mod_norm(x_ref[...], g_ref[...], scale_ref[...], shift_ref[...]).astype(BF16)
    qc = (_rms(_dot(h, wdq_ref[...])) * gq_ref[...]).astype(BF16)
    qn = _dot(qc, wn_ref[...])
    qrt = (_dot_nt(wrt_ref[...], qc) * cos_ref[...] + _dot_nt(wrrt_ref[...], qc) * sin_ref[...]) * SM_SCALE
    for hd in range(MLA_HEADS):
        cols = slice(hd * tm, (hd + 1) * tm)
        qlt = _dot_nt(wukt_ref[hd], qn[:, hd * QK_NOPE:(hd + 1) * QK_NOPE].astype(BF16)) * SM_SCALE
        q_ref[0:KV_RANK, cols] = qlt.astype(q_ref.dtype)
        q_ref[KV_RANK:QK_LAT, cols] = qrt[hd * QK_ROPE:(hd + 1) * QK_ROPE, :].astype(q_ref.dtype)


def _mla_q(grp, x, mods, g_norm_row, w_dq, g_q_row, w_n, w_rt, w_rrt, w_ukt, cos_t, sin_t, layer_b):
    sub = 1
    nseq = grp.rows // grp.seq
    tps = grp.tiles_per_seq
    width = MLA_HEADS * QK_ROPE
    sel = lambda a: pl.BlockSpec((None,) + a.shape[1:], lambda i: (layer_b,) + (0,) * (a.ndim - 1))
    if grp.per_row:
        pos = pl.BlockSpec((width, grp.tm), lambda i: (0, i))
    else:
        pos = pl.BlockSpec((width, grp.tm), lambda i: (0, i % tps))
    return pl.pallas_call(
        _mla_q_body,
        out_shape=jax.ShapeDtypeStruct((nseq, tps, QK_LAT, MLA_HEADS * grp.tm), BF16),
        grid=(grp.rows // grp.tm,),
        in_specs=[_row_spec(grp, D_MODEL),
                  _mod_spec(grp, mods, sub * 3 + 0),
                  _mod_spec(grp, mods, sub * 3 + 1),
                  _full_spec((1, D_MODEL)),
                  sel(w_dq), sel(g_q_row), sel(w_n), sel(w_rt), sel(w_rrt),
                  _full_spec(w_ukt.shape),
                  pos, pos],
        out_specs=pl.BlockSpec((None, None, QK_LAT, MLA_HEADS * grp.tm), lambda i: (i // tps, i % tps, 0, 0)),
        compiler_params=_params("parallel"),
        name="mla_q",
    )(x, mods, mods, g_norm_row, w_dq, g_q_row, w_n, w_rt, w_rrt, w_ukt, cos_t, sin_t)


def _attn_body(qi_ref, kj_ref, qt_ref, k_ref, vt_ref, o_ref, m_scr, acc_scr, *, tq):
    p = pl.program_id(1)
    i = qi_ref[p]
    j = kj_ref[p]

    @pl.when(j == 0)
    def _():
        m_scr[...] = jnp.full_like(m_scr, -jnp.inf)
        acc_scr[...] = jnp.zeros_like(acc_scr)

    def update(masked):
        s = _dot(k_ref[...], qt_ref[...])
        if masked:
            kpos = lax.broadcasted_iota(jnp.int32, s.shape, 0)
            qpos = lax.broadcasted_iota(jnp.int32, s.shape, 1) & (tq - 1)
            s = jnp.where(kpos <= qpos, s, -jnp.inf)
        m_prev = m_scr[...]
        m_new = jnp.maximum(m_prev, jnp.max(s, axis=0, keepdims=True))
        alpha = jnp.exp(m_prev - m_new)
        pr = jnp.exp(s - m_new).astype(BF16)
        acc_scr[...] = alpha * acc_scr[...] + _dot(vt_ref[...], pr)
        m_scr[...] = m_new

    @pl.when(j < i)
    def _():
        update(False)

    @pl.when(j == i)
    def _():
        update(True)
        out = acc_scr[0:KV_RANK, :] / acc_scr[KV_RANK:KV_RANK + 1, :]
        for hd in range(MLA_HEADS):
            o_ref[:, hd * KV_RANK:(hd + 1) * KV_RANK] = out[:, hd * tq:(hd + 1) * tq].T.astype(o_ref.dtype)


def _attend_prompt(qt, kcat, ckvt, batch, seq, tq):
    nq = seq // tq
    pairs = [(i, j) for i in range(nq) for j in range(i + 1)]
    qi = jnp.asarray(np.array([p[0] for p in pairs], np.int32))
    kj = jnp.asarray(np.array([p[1] for p in pairs], np.int32))
    cols = MLA_HEADS * tq
    return pl.pallas_call(
        functools.partial(_attn_body, tq=tq),
        out_shape=jax.ShapeDtypeStruct((batch, seq, MLA_HEADS * KV_RANK), BF16),
        grid_spec=pltpu.PrefetchScalarGridSpec(
            num_scalar_prefetch=2,
            grid=(batch, len(pairs)),
            in_specs=[pl.BlockSpec((None, None, QK_LAT, cols), lambda b, p, qi, kj: (b, qi[p], 0, 0)),
                      pl.BlockSpec((None, tq, QK_LAT), lambda b, p, qi, kj: (b, kj[p], 0)),
                      pl.BlockSpec((None, VT_ROWS, tq), lambda b, p, qi, kj: (b, 0, kj[p]))],
            out_specs=pl.BlockSpec((None, tq, MLA_HEADS * KV_RANK), lambda b, p, qi, kj: (b, qi[p], 0)),
            scratch_shapes=[pltpu.VMEM((1, cols), F32), pltpu.VMEM((VT_ROWS, cols), F32)],
        ),
        compiler_params=_params("parallel", "arbitrary"),
        name="attend_prompt",
    )(qi, kj, qt, kcat.reshape(batch, seq, QK_LAT), ckvt)


PAGED_CHUNK_PAGES = 32
PAGED_SLOTS = 4


def _paged_body(pt_ref, q_ref, cn_ref, kn_ref, ckv_hbm, krt_hbm, o_ref, cbuf, kbuf, sems, *, n_pages):
    ch = PAGED_CHUNK_PAGES
    nc = n_pages // ch
    ahead = PAGED_SLOTS - 1
    nseq = pl.num_programs(0)
    b = pl.program_id(0)

    def copies(g, slot):
        out = []
        for pg in range(ch):
            page = pt_ref[g * ch + pg]
            dst = pl.ds(pg * PAGE_SIZE, PAGE_SIZE)
            out.append(pltpu.make_async_copy(ckv_hbm.at[page], cbuf.at[slot, dst, :], sems.at[0, slot]))
            out.append(pltpu.make_async_copy(krt_hbm.at[page], kbuf.at[slot, pg], sems.at[1, slot]))
        return out

    def start(g, slot):
        for cp in copies(g, slot):
            cp.start()

    def wait(slot):
        for cp in copies(0, slot):
            cp.wait()

    @pl.when(b == 0)
    def _():
        for c in range(ahead):
            start(c, c)

    q = q_ref[...]
    ql = q[:, 0:KV_RANK]
    qr = q[:, KV_RANK:QK_LAT]
    cn = cn_ref[...]
    kn = kn_ref[...]
    m = jnp.sum(ql * cn, axis=1, keepdims=True) + jnp.sum(qr * kn, axis=1, keepdims=True)
    l = jnp.ones_like(m)
    acc = jnp.broadcast_to(cn, (MLA_HEADS, KV_RANK))
    qlb = ql.astype(BF16)
    qrb = qr.astype(BF16)
    for c in range(nc):
        slot = c % PAGED_SLOTS
        g = b * nc + c
        nxt = (c + ahead) % PAGED_SLOTS
        if c + ahead < nc:
            start(g + ahead, nxt)
        else:
            @pl.when(b + 1 < nseq)
            def _():
                start(g + ahead, nxt)
        wait(slot)
        kc = cbuf[slot].astype(BF16)
        s_rope = [_dot(qrb, kbuf[slot, pg].astype(BF16)) for pg in range(ch)]
        s = _dot_nt(qlb, kc) + jnp.concatenate(s_rope, axis=1)
        m_new = jnp.maximum(m, jnp.max(s, axis=1, keepdims=True))
        alpha = jnp.exp(m - m_new)
        pr = jnp.exp(s - m_new)
        l = alpha * l + jnp.sum(pr, axis=1, keepdims=True)
        acc = alpha * acc + _dot(pr.astype(BF16), kc)
        m = m_new
    o_ref[...] = acc / l


def _attend_paged(q, ckv_new, krope_new, cache_ckv, cache_krope_t, page_table):
    nseq, n_pages = page_table.shape
    ch = PAGED_CHUNK_PAGES
    assert n_pages % (PAGED_SLOTS * ch) == 0
    per_seq = lambda *tail: pl.BlockSpec((None,) + tail, lambda i, pt: (i,) + (0,) * len(tail))
    return pl.pallas_call(
        functools.partial(_paged_body, n_pages=n_pages),
        out_shape=jax.ShapeDtypeStruct((nseq, MLA_HEADS, KV_RANK), F32),
        grid_spec=pltpu.PrefetchScalarGridSpec(
            num_scalar_prefetch=1,
            grid=(nseq,),
            in_specs=[per_seq(MLA_HEADS, QK_LAT), per_seq(1, KV_RANK), per_seq(1, QK_ROPE),
                      pl.BlockSpec(memory_space=pl.ANY), pl.BlockSpec(memory_space=pl.ANY)],
            out_specs=per_seq(MLA_HEADS, KV_RANK),
            scratch_shapes=[pltpu.VMEM((PAGED_SLOTS, ch * PAGE_SIZE, KV_RANK), F32),
                            pltpu.VMEM((PAGED_SLOTS, ch, QK_ROPE, PAGE_SIZE), F32),
                            pltpu.SemaphoreType.DMA((2, PAGED_SLOTS))],
        ),
        compiler_params=_params("arbitrary"),
        name="attend_paged",
    )(page_table.reshape(-1), q, ckv_new.reshape(nseq, 1, KV_RANK), krope_new.reshape(nseq, 1, QK_ROPE),
      cache_ckv, cache_krope_t)


def _attn_out_body(o_ref, wuv_ref, wo_ref, x_ref, gate_ref, out_ref):
    parts = [_dot(o_ref[:, hd * KV_RANK:(hd + 1) * KV_RANK], wuv_ref[hd]).astype(BF16)
             for hd in range(MLA_HEADS)]
    o = jnp.concatenate(parts, axis=1)
    out_ref[...] = x_ref[...] + gate_ref[...] * _dot(o, wo_ref[...])


def _attn_out(grp, o_lat, w_uv_t, w_out, layer_b, x, mods, sub):
    width = MLA_HEADS * KV_RANK
    return pl.pallas_call(
        _attn_out_body,
        out_shape=jax.ShapeDtypeStruct((grp.rows, D_MODEL), F32),
        grid=(grp.rows // grp.tm,),
        in_specs=[_row_spec(grp, width),
                  _full_spec(w_uv_t.shape),
                  pl.BlockSpec((None, MLA_HEADS * V_HEAD, D_MODEL), lambda i: (layer_b, 0, 0)),
                  _row_spec(grp, D_MODEL),
                  _mod_spec(grp, mods, sub * 3 + 2)],
        out_specs=_row_spec(grp, D_MODEL),
        compiler_params=_params("parallel"),
        name="attn_out",
    )(o_lat, w_uv_t, w_out, x, mods)


def _rope_tables(pos):
    half = QK_ROPE // 2
    inv_freq = ROPE_THETA ** (-2.0 * jnp.arange(half, dtype=F32) / QK_ROPE)
    ang = pos.astype(F32)[:, None] * inv_freq[None, :]
    cos, sin = jnp.cos(ang), jnp.sin(ang)
    return jnp.concatenate([cos, cos], axis=-1), jnp.concatenate([-sin, sin], axis=-1)


def _swap_halves(w):
    half = QK_ROPE // 2
    return jnp.concatenate([w[..., half:], w[..., :half]], axis=-1)


def kernel(x_prompt, x_sample, c_prompt, c_sample, state_hgrn, cache_ckv, cache_krope, page_table, w_ada, b_ada, g_norm, w_ffn_gu, w_ffn_down, hg_w_in, hg_lb, hg_g_out, hg_w_out, kv_w_ada, kv_b_ada, kv_g_norm, kv_w_down, kv_g_latent, kv_w_uk, kv_w_uv, q_w_down, q_g_norm, q_w_up, attn_w_out, g_final):
    batch, seq, d = x_prompt.shape
    nseq = x_sample.shape[0]
    n_b = DEPTH - N_A_LAYERS

    w_in = hg_w_in.astype(BF16)
    w_ho = hg_w_out.astype(BF16)
    w_ao = attn_w_out.astype(BF16)
    w_kc = kv_w_down[:, :KV_RANK].astype(BF16)
    w_kr = kv_w_down[:, KV_RANK:].astype(BF16)
    w_krr = _swap_halves(kv_w_down[:, KV_RANK:]).astype(BF16)
    w_dq = q_w_down.astype(BF16)
    w_qn = q_w_up[..., :QK_NOPE].reshape(n_b, Q_RANK, MLA_HEADS * QK_NOPE).astype(BF16)
    w_qr = q_w_up[..., QK_NOPE:].reshape(n_b, Q_RANK, MLA_HEADS * QK_ROPE)
    w_qrr = _swap_halves(q_w_up[..., QK_NOPE:]).reshape(n_b, Q_RANK, MLA_HEADS * QK_ROPE)
    w_qrt = w_qr.transpose(0, 2, 1).astype(BF16)
    w_qrrt = w_qrr.transpose(0, 2, 1).astype(BF16)
    w_ukt = kv_w_uk.transpose(1, 0, 2).astype(BF16)
    w_uv_t = kv_w_uv.transpose(1, 0, 2).astype(BF16)
    cache_krope_t = cache_krope.transpose(0, 2, 1)
    g_q_rows = q_g_norm.reshape(n_b, 1, Q_RANK)
    g_out_rows = hg_g_out.reshape(N_A_LAYERS, 1, D_MODEL)

    pad = 8
    c_all = jnp.concatenate([c_prompt, jnp.zeros((pad - batch, d), F32), c_sample], axis=0)
    mods_all = [_ada(c_all, w_ada, b_ada.reshape(DEPTH, 1, -1), l) for l in range(DEPTH)]
    kvm_all = _ada(c_all, kv_w_ada[None], kv_b_ada.reshape(1, 1, -1), 0)

    def trunk(grp, x, mods_l, kvmods, pos, mix, attend, stream_dtype):
        grp_half = grp._replace(tm=min(grp.tm, WIDE_TM))
        grp_q = grp._replace(tm=min(grp.tm, ATTN_TQ))
        cos_k, sin_k = _rope_tables(pos)
        cos_qt, sin_qt = jnp.tile(cos_k, (1, MLA_HEADS)).T, jnp.tile(sin_k, (1, MLA_HEADS)).T
        states = []
        ckv = krope = kcat = ckvt = None
        for l in range(DEPTH):
            mods = mods_l[l]
            gn = lambda j: g_norm[l, j].reshape(1, d)
            if l == N_A_LAYERS:
                ckv, krope, kcat, ckvt = _shared_kv(grp, x, kvmods, kv_g_norm.reshape(1, d), w_kc, w_kr, w_krr,
                                                    kv_g_latent.reshape(1, KV_RANK), cos_k, sin_k)
            x = _ffn(grp, x, mods, 0, gn(0), w_ffn_gu, w_ffn_down, l, 0)
            if l < N_A_LAYERS:
                qs, k, v, og, lg = _hgrn_in(grp_half, x, mods, gn(1), w_in, hg_lb, l, stream_dtype)
                y, s = mix(l, qs, k, v, og, lg)
                states.append(s)
                x = _proj_res(grp, y, w_ho, l, x, mods, 1)
            else:
                lb_ = l - N_A_LAYERS
                qt = _mla_q(grp_q, x, mods, gn(1), w_dq, g_q_rows, w_qn, w_qrt, w_qrrt, w_ukt, cos_qt, sin_qt,
                            lb_)
                o_lat = attend(qt, ckv, krope, kcat, ckvt)
                x = _attn_out(grp, o_lat, w_uv_t, w_ao, lb_, x, mods, 1)
            x = _ffn(grp, x, mods, 2, gn(2), w_ffn_gu, w_ffn_down, l, 1,
                     g_final=g_final.reshape(1, d) if l == DEPTH - 1 else None)
        return x, jnp.stack(states), ckv, krope

    grp_p = Group(rows=batch * seq, seq=seq, tm=PROMPT_TM, per_row=False)
    mods_p = [m[:pad].reshape(pad, 1, -1) for m in mods_all]
    kvm_p = kvm_all[:pad].reshape(pad, 1, -1)

    def mix_p(l, qs, k, v, og, lg):
        y, s = _gla_prompt(qs, k, v, og, lg, g_out_rows[l], batch, seq)
        return y.reshape(batch * seq, d), s

    def attend_p(qt, ckv, krope, kcat, ckvt):
        o = _attend_prompt(qt, kcat, ckvt, batch, seq, ATTN_TQ)
        return o.reshape(batch * seq, MLA_HEADS * KV_RANK)

    y_p, st_p, ckv_p, kr_p = trunk(grp_p, x_prompt.reshape(batch * seq, d), mods_p, kvm_p,
                                   jnp.arange(seq), mix_p, attend_p, BF16)

    assert nseq == MIN_TM
    grp_s = Group(rows=nseq, seq=nseq, tm=nseq, per_row=True)
    mods_s = [m[pad:] for m in mods_all]
    kvm_s = kvm_all[pad:]

    def mix_s(l, qs, k, v, og, lg):
        y, s = _gla_step(qs, k, v, og, lg, hg_g_out[l].reshape(HG_HEADS, HG_DV), state_hgrn, l)
        return y.reshape(nseq, d), s

    def attend_s(qt, ckv, krope, kcat, ckvt):
        qf = qt.reshape(QK_LAT, MLA_HEADS, nseq).transpose(2, 1, 0).astype(F32)
        o = _attend_paged(qf, ckv, krope, cache_ckv, cache_krope_t, page_table)
        return o.reshape(nseq, MLA_HEADS * KV_RANK).astype(BF16)

    y_s, st_s, ckv_s, kr_s = trunk(grp_s, x_sample.reshape(nseq, d), mods_s, kvm_s,
                                   jnp.full((nseq,), PAST_LEN), mix_s, attend_s, F32)

    return (y_p.reshape(batch, seq, d), y_s.reshape(nseq, 1, d),
            st_p.astype(state_hgrn.dtype), st_s.astype(state_hgrn.dtype),
            ckv_p.reshape(batch, seq, KV_RANK), kr_p.reshape(batch, seq, QK_ROPE),
            ckv_s.reshape(nseq, 1, KV_RANK), kr_s.reshape(nseq, 1, QK_ROPE))
```

```python
import functools
from typing import NamedTuple

import jax
import jax.numpy as jnp
import numpy as np
from jax import lax
from jax.experimental import pallas as pl
from jax.experimental.pallas import tpu as pltpu

D_MODEL = 1024
DEPTH = 2
N_A_LAYERS = DEPTH // 2
PAST_LEN = 16384
PAGE_SIZE = 128
D_FF = 2816
HG_HEADS = 8
HG_DK = D_MODEL // HG_HEADS
HG_DV = D_MODEL // HG_HEADS
MLA_HEADS = 8
QK_NOPE = 128
QK_ROPE = 64
V_HEAD = 128
KV_RANK = 256
Q_RANK = 384
QK_LAT = KV_RANK + QK_ROPE
ROPE_THETA = 10000.0
EPS = 1e-6
SM_SCALE = (QK_NOPE + QK_ROPE) ** -0.5

BF16 = jnp.bfloat16
F32 = jnp.float32

V7X_VMEM_BYTES = 64 * 1024 * 1024
VMEM_LIMIT = V7X_VMEM_BYTES * 3 // 4

PROMPT_TM = 1024
WIDE_TM = 512
MIN_TM = 128
ADA_TN = 1024
STEP_SEQS = 8

GLA_CHUNK = 64
GLA_SUB = 16
GLA_BLOCK = 512
GLA_SAFE_DECAY = 60.0

ATTN_TQ = 512
VT_ROWS = KV_RANK + 16

NT_DIMS = (((1,), (1,)), ((), ()))


class Group(NamedTuple):
    rows: int
    seq: int
    tm: int
    per_row: bool

    @property
    def tiles_per_seq(self):
        return self.seq // self.tm


def _params(*sem):
    return pltpu.CompilerParams(dimension_semantics=sem, vmem_limit_bytes=VMEM_LIMIT)


def _dot(a, b):
    return jnp.dot(a, b, preferred_element_type=F32)


def _dot_nt(a, b):
    return lax.dot_general(a, b, NT_DIMS, preferred_element_type=F32)


def _sigmoid(x):
    return 1.0 / (1.0 + jnp.exp(-x))


def _rms(x):
    return x * lax.rsqrt(jnp.mean(x * x, axis=-1, keepdims=True) + EPS)


def _mod_norm(x, g, scale, shift):
    return (_rms(x) * g) * (1.0 + scale) + shift


def _mod_spec(grp, mods, col):
    if grp.per_row:
        return pl.BlockSpec((grp.tm, D_MODEL), lambda *g: (g[0], col))
    tps = grp.tiles_per_seq
    return pl.BlockSpec((None, 1, D_MODEL), lambda *g: (g[0] // tps, 0, col))


def _pos_spec(grp, width):
    if grp.per_row:
        return pl.BlockSpec((grp.tm, width), lambda *g: (g[0], 0))
    tps = grp.tiles_per_seq
    return pl.BlockSpec((grp.tm, width), lambda *g: (g[0] % tps, 0))


def _row_spec(grp, width):
    return pl.BlockSpec((grp.tm, width), lambda *g: (g[0], 0))


def _full_spec(shape):
    nd = len(shape)
    return pl.BlockSpec(shape, lambda *g: (0,) * nd)


def _ada_body(c_ref, w_ref, b_ref, o_ref):
    c = c_ref[...]
    sc = (c * _sigmoid(c)).astype(BF16)
    o_ref[...] = _dot(sc, w_ref[...].astype(BF16)) + b_ref[...]


def _ada(c_all, w, b, layer, tn=ADA_TN):
    rows = c_all.shape[0]
    n = w.shape[-1]
    return pl.pallas_call(
        _ada_body,
        out_shape=jax.ShapeDtypeStruct((rows, n), F32),
        grid=(n // tn,),
        in_specs=[
            _full_spec((rows, D_MODEL)),
            pl.BlockSpec((None, D_MODEL, tn), lambda j: (layer, 0, j)),
            pl.BlockSpec((None, 1, tn), lambda j: (layer, 0, j)),
        ],
        out_specs=pl.BlockSpec((rows, tn), lambda j: (0, j)),
        compiler_params=_params("arbitrary"),
        name="ada_proj",
    )(c_all, w, b)


FFN_TF = 256
FFN_GROUP = 3
FFN_ROW_PARTS = 2


def _ffn_body(x_ref, shift_ref, scale_ref, gate_ref, g_ref, wgu_hbm, wd_hbm, *rest, layer, half, final_norm):
    if final_norm:
        gf_ref, o_ref, h_scr, acc_scr, wg_buf, wu_buf, wd_buf, sems = rest
    else:
        o_ref, h_scr, acc_scr, wg_buf, wu_buf, wd_buf, sems = rest
    tf = FFN_TF
    nf = D_FF // tf
    groups = [list(range(s, min(s + FFN_GROUP, nf))) for s in range(0, nf, FFN_GROUP)]
    ng = len(groups)
    assert ng % 2 == 0
    i = pl.program_id(0)
    ni = pl.num_programs(0)

    def copies(g, slot):
        out = []
        for t, j in enumerate(groups[g]):
            out += [
                pltpu.make_async_copy(wgu_hbm.at[layer, half, :, pl.ds(j * tf, tf)], wg_buf.at[slot, t],
                                      sems.at[0, slot]),
                pltpu.make_async_copy(wgu_hbm.at[layer, half, :, pl.ds((nf + j) * tf, tf)], wu_buf.at[slot, t],
                                      sems.at[1, slot]),
                pltpu.make_async_copy(wd_hbm.at[layer, half, pl.ds(j * tf, tf), :], wd_buf.at[slot, t],
                                      sems.at[2, slot]),
            ]
        return out

    def start(g, slot):
        for cp in copies(g, slot):
            cp.start()

    def wait(g, slot):
        for cp in copies(g, slot):
            cp.wait()

    @pl.when(i == 0)
    def _():
        start(0, 0)

    tm = x_ref.shape[0]
    n_parts = FFN_ROW_PARTS if tm % (FFN_ROW_PARTS * 16) == 0 and tm > MIN_TM else 1
    parts = [slice(p * tm // n_parts, (p + 1) * tm // n_parts) for p in range(n_parts)]
    rows_of = lambda ref, p: ref[p, :] if ref.shape[0] == tm else ref[...]

    for g in range(ng):
        slot = g % 2
        wait(g, slot)
        for t, j in enumerate(groups[g]):
            wg = wg_buf[slot, t].astype(BF16)
            wu = wu_buf[slot, t].astype(BF16)
            wd = wd_buf[slot, t].astype(BF16)
            if t == 0:
                if g + 1 < ng:
                    start(g + 1, 1 - slot)
                else:
                    @pl.when(i + 1 < ni)
                    def _():
                        start(0, 1 - slot)
            for p in parts:
                if j == 0:
                    h_scr[p, :] = _mod_norm(x_ref[p, :], g_ref[...], rows_of(scale_ref, p),
                                            rows_of(shift_ref, p)).astype(BF16)
                h = h_scr[p, :]
                a = _dot(h, wg)
                b = _dot(h, wu)
                act = ((a * _sigmoid(a)) * b).astype(BF16)
                part = _dot(act, wd)
                if j == 0:
                    acc_scr[p, :] = part
                else:
                    acc_scr[p, :] += part

    y = x_ref[...] + (0.5 * gate_ref[...]) * acc_scr[...]
    if final_norm:
        y = _rms(y) * gf_ref[...]
    o_ref[...] = y


def _ffn(grp, x, mods, sub, g_norm_row, w_gu, w_down, layer, half, g_final=None):
    tf = FFN_TF
    final_norm = g_final is not None
    in_specs = [
        _row_spec(grp, D_MODEL),
        _mod_spec(grp, mods, sub * 3 + 0),
        _mod_spec(grp, mods, sub * 3 + 1),
        _mod_spec(grp, mods, sub * 3 + 2),
        _full_spec((1, D_MODEL)),
        pl.BlockSpec(memory_space=pl.ANY),
        pl.BlockSpec(memory_space=pl.ANY),
    ]
    args = [x, mods, mods, mods, g_norm_row, w_gu, w_down]
    if final_norm:
        in_specs.append(_full_spec((1, D_MODEL)))
        args.append(g_final)
    return pl.pallas_call(
        functools.partial(_ffn_body, layer=layer, half=half, final_norm=final_norm),
        out_shape=jax.ShapeDtypeStruct((grp.rows, D_MODEL), F32),
        grid=(grp.rows // grp.tm,),
        in_specs=in_specs,
        out_specs=_row_spec(grp, D_MODEL),
        scratch_shapes=[pltpu.VMEM((grp.tm, D_MODEL), BF16), pltpu.VMEM((grp.tm, D_MODEL), F32),
                        pltpu.VMEM((2, FFN_GROUP, D_MODEL, tf), F32), pltpu.VMEM((2, FFN_GROUP, D_MODEL, tf), F32),
                        pltpu.VMEM((2, FFN_GROUP, tf, D_MODEL), F32), pltpu.SemaphoreType.DMA((3, 2))],
        compiler_params=_params("arbitrary"),
        name="ffn_half",
    )(*args)


def _hgrn_in_body(x_ref, shift_ref, scale_ref, g_ref, w_ref, lb_ref, qs_ref, k_ref, v_ref, og_ref, lg_ref,
                  *, layer):
    d = D_MODEL
    h = _mod_norm(x_ref[...], g_ref[...], scale_ref[...], shift_ref[...]).astype(BF16)
    lbp = lb_ref[...]
    e = jnp.exp(lbp - jnp.max(lbp, axis=0, keepdims=True))
    lb = jnp.sum(e[: layer + 1], axis=0, keepdims=True) / jnp.sum(e, axis=0, keepdims=True)

    f = _dot(h, w_ref[:, d:2 * d])
    q = _dot(h, w_ref[:, 0:d])
    g = _dot(h, w_ref[:, 3 * d:4 * d])
    v_ref[...] = _dot(h, w_ref[:, 2 * d:3 * d]).astype(v_ref.dtype)
    u = jnp.exp(-jnp.abs(f))
    r = 1.0 / (1.0 + u)
    ur = u * r
    pos = f >= 0.0
    lg_ref[...] = jnp.log(lb + (1.0 - lb) * jnp.where(pos, r, ur))
    k_ref[...] = ((1.0 - lb) * jnp.where(pos, ur, r)).astype(k_ref.dtype)
    qs_ref[...] = (q * _sigmoid(q)).astype(qs_ref.dtype)
    og_ref[...] = (g * _sigmoid(g)).astype(og_ref.dtype)


def _hgrn_in(grp, x, mods, g_norm_row, w_in, hg_lb, layer, stream_dtype):
    sub = 1
    outs = ([jax.ShapeDtypeStruct((grp.rows, D_MODEL), stream_dtype)] * 4
            + [jax.ShapeDtypeStruct((grp.rows, D_MODEL), F32)])
    return pl.pallas_call(
        functools.partial(_hgrn_in_body, layer=layer),
        out_shape=outs,
        grid=(grp.rows // grp.tm,),
        in_specs=[
            _row_spec(grp, D_MODEL),
            _mod_spec(grp, mods, sub * 3 + 0),
            _mod_spec(grp, mods, sub * 3 + 1),
            _full_spec((1, D_MODEL)),
            pl.BlockSpec((None, D_MODEL, 4 * D_MODEL), lambda i: (layer, 0, 0)),
            _full_spec(hg_lb.shape),
        ],
        out_specs=[_row_spec(grp, D_MODEL)] * 5,
        compiler_params=_params("parallel"),
        name="hgrn_in",
    )(x, mods, mods, g_norm_row, w_in, hg_lb)


def _segmented_cumsum(x, seg):
    row = lax.broadcasted_iota(jnp.int32, x.shape, 0) & (seg - 1)
    s = 1
    while s < seg:
        x = x + jnp.where(row >= s, pltpu.roll(x, s, 0), 0.0)
        s *= 2
    return x


def _gla_body(q_ref, k_ref, v_ref, og_ref, lg_ref, go_ref, y_ref, s_out_ref, st_scr, b_scr):
    c, sub = GLA_CHUNK, GLA_SUB
    t = pl.program_id(1)

    @pl.when(t == 0)
    def _():
        st_scr[...] = jnp.zeros_like(st_scr)

    b_scr[...] = _segmented_cumsum(lg_ref[...], c)
    tri = lax.broadcasted_iota(jnp.int32, (c, c), 1) <= lax.broadcasted_iota(jnp.int32, (c, c), 0)

    def finish(r0, h, o):
        cols = slice(h * HG_DV, (h + 1) * HG_DV)
        y = _rms(o) * go_ref[:, cols] * og_ref[pl.ds(r0, c), cols].astype(F32)
        y_ref[pl.ds(r0, c), cols] = y.astype(y_ref.dtype)

    n_chunks = q_ref.shape[0] // c

    def chunk_refs(r0):
        b_all = b_scr[pl.ds(r0, c), :]
        return b_all, [jnp.zeros((1, D_MODEL), F32)] + [b_all[i * sub - 1:i * sub, :] for i in range(1, c // sub)]

    worst = jnp.zeros((1, D_MODEL), F32)
    for ci in range(n_chunks):
        b_all, ref_rows = chunk_refs(ci * c)
        for i in range(c // sub):
            worst = jnp.maximum(worst, ref_rows[i] - b_all[(i + 1) * sub - 1:(i + 1) * sub, :])
    safe = jnp.max(worst) <= GLA_SAFE_DECAY

    @pl.when(safe)
    def _():
        for ci in range(n_chunks):
            r0 = ci * c
            b_all, ref_rows = chunk_refs(r0)
            stage = []
            for h in range(HG_HEADS):
                cols = slice(h * HG_DK, (h + 1) * HG_DK)
                b = b_all[:, cols]
                b_last = b[c - 1:c, :]
                refs = [r[:, cols] for r in ref_rows]
                q = q_ref[pl.ds(r0, c), cols].astype(F32)
                k = k_ref[pl.ds(r0, c), cols].astype(F32)
                v = v_ref[pl.ds(r0, c), cols]
                st = st_scr[h]
                o_inter = _dot_nt((q * jnp.exp(b)).astype(BF16), st.astype(BF16))
                rows = []
                for i in range(c // sub):
                    qi = (q[i * sub:(i + 1) * sub] * jnp.exp(b[i * sub:(i + 1) * sub] - refs[i])).astype(BF16)
                    ki = (k * jnp.exp(jnp.minimum(refs[i] - b, GLA_SAFE_DECAY))).astype(BF16)
                    rows.append(_dot_nt(qi, ki))
                kd = (k * jnp.exp(b_last - b)).astype(BF16)
                vt = v.astype(F32).T.astype(BF16)
                st_scr[h] = st * jnp.exp(b_last) + _dot(vt, kd)
                stage.append((o_inter, rows, v))
            for h, (o_inter, rows, v) in enumerate(stage):
                att = jnp.where(tri, jnp.concatenate(rows, axis=0), 0.0)
                finish(r0, h, o_inter + _dot(att.astype(BF16), v))

    @pl.when(jnp.logical_not(safe))
    def _():
        lane = lax.broadcasted_iota(jnp.int32, (HG_DV, c), 1)
        sublane = lax.broadcasted_iota(jnp.int32, (c, HG_DK), 0)

        def slow_chunk(ci, carry):
            r0 = pl.multiple_of(ci * c, c)
            for h in range(HG_HEADS):
                cols = slice(h * HG_DK, (h + 1) * HG_DK)
                vt = v_ref[pl.ds(r0, c), cols].astype(F32).T
                qf = q_ref[pl.ds(r0, c), cols].astype(F32)
                kf = k_ref[pl.ds(r0, c), cols].astype(F32)
                df = jnp.exp(lg_ref[pl.ds(r0, c), cols])

                def token(ti, carry2, vt=vt, qf=qf, kf=kf, df=df):
                    st, ot = carry2
                    sel = lane == ti
                    pick = lambda a: jnp.sum(jnp.where(sublane == ti, a, 0.0), axis=0, keepdims=True)
                    v_col = jnp.sum(jnp.where(sel, vt, 0.0), axis=1, keepdims=True)
                    st = st * pick(df) + v_col * pick(kf)
                    o_col = jnp.sum(st * pick(qf), axis=1, keepdims=True)
                    return st, jnp.where(sel, o_col, ot)

                st, ot = lax.fori_loop(0, c, token, (st_scr[h], jnp.zeros((HG_DV, c), F32)))
                st_scr[h] = st
                finish(r0, h, ot.T)
            return carry

        lax.fori_loop(0, n_chunks, slow_chunk, 0)

    @pl.when(t == pl.num_programs(1) - 1)
    def _():
        for h in range(HG_HEADS):
            s_out_ref[h] = st_scr[h].T


def _gla_prompt(qs, k, v, og, lg, g_out_row, batch, seq):
    tb = GLA_BLOCK
    stream = pl.BlockSpec((None, tb, D_MODEL), lambda b, t: (b, t, 0))
    shp = (batch, seq, D_MODEL)
    return pl.pallas_call(
        _gla_body,
        out_shape=[jax.ShapeDtypeStruct(shp, BF16),
                   jax.ShapeDtypeStruct((batch, HG_HEADS, HG_DK, HG_DV), F32)],
        grid=(batch, seq // tb),
        in_specs=[stream, stream, stream, stream, stream, _full_spec((1, D_MODEL))],
        out_specs=[stream, pl.BlockSpec((None, HG_HEADS, HG_DK, HG_DV), lambda b, t: (b, 0, 0, 0))],
        scratch_shapes=[pltpu.VMEM((HG_HEADS, HG_DV, HG_DK), F32), pltpu.VMEM((tb, D_MODEL), F32)],
        compiler_params=_params("parallel", "arbitrary"),
        name="gla_chunked",
    )(qs.reshape(shp), k.reshape(shp), v.reshape(shp), og.reshape(shp), lg.reshape(shp), g_out_row)


def _gla_step_body(q_ref, k_ref, v_ref, og_ref, lg_ref, go_ref, s_ref, y_ref, s_out_ref):
    go = go_ref[...]

    def one(bi, carry):
        qt = q_ref[bi].astype(F32).T
        kt = k_ref[bi].astype(F32).T
        dt = jnp.exp(lg_ref[bi]).T
        v = v_ref[bi].astype(F32)
        outs = []
        for h in range(HG_HEADS):
            s = s_ref[bi, h] * dt[:, h:h + 1] + kt[:, h:h + 1] * v[h:h + 1, :]
            s_out_ref[bi, h] = s
            outs.append(jnp.sum(s * qt[:, h:h + 1], axis=0, keepdims=True))
        o = jnp.concatenate(outs, axis=0)
        y_ref[bi] = (_rms(o) * go * og_ref[bi].astype(F32)).astype(y_ref.dtype)
        return carry

    lax.fori_loop(0, q_ref.shape[0], one, 0)


def _gla_step(qs, k, v, og, lg, g_out_heads, state, layer, bb=STEP_SEQS):
    rows = qs.shape[0]
    shp = (rows, HG_HEADS, HG_DK)
    stream = pl.BlockSpec((bb, HG_HEADS, HG_DK), lambda i: (i, 0, 0))
    return pl.pallas_call(
        _gla_step_body,
        out_shape=[jax.ShapeDtypeStruct(shp, F32),
                   jax.ShapeDtypeStruct((rows, HG_HEADS, HG_DK, HG_DV), F32)],
        grid=(rows // bb,),
        in_specs=[stream, stream, stream, stream, stream, _full_spec((HG_HEADS, HG_DV)),
                  pl.BlockSpec((None, bb, HG_HEADS, HG_DK, HG_DV), lambda i: (layer, i, 0, 0, 0))],
        out_specs=[stream, pl.BlockSpec((bb, HG_HEADS, HG_DK, HG_DV), lambda i: (i, 0, 0, 0))],
        compiler_params=_params("parallel"),
        name="gla_step",
    )(qs.reshape(shp), k.reshape(shp), v.reshape(shp), og.reshape(shp), lg.reshape(shp), g_out_heads, state)


def _proj_res_body(a_ref, w_ref, x_ref, gate_ref, o_ref):
    o_ref[...] = x_ref[...] + gate_ref[...] * _dot(a_ref[...].astype(BF16), w_ref[...])


def _proj_res(grp, a, w, layer, x, mods, sub):
    kdim = a.shape[1]
    return pl.pallas_call(
        _proj_res_body,
        out_shape=jax.ShapeDtypeStruct((grp.rows, D_MODEL), F32),
        grid=(grp.rows // grp.tm,),
        in_specs=[_row_spec(grp, kdim),
                  pl.BlockSpec((None, kdim, D_MODEL), lambda i: (layer, 0, 0)),
                  _row_spec(grp, D_MODEL),
                  _mod_spec(grp, mods, sub * 3 + 2)],
        out_specs=_row_spec(grp, D_MODEL),
        compiler_params=_params("parallel"),
        name="proj_residual",
    )(a, w, x, mods)


def _kv_body(x_ref, shift_ref, scale_ref, g_ref, wc_ref, wr_ref, wrr_ref, gl_ref, cos_ref, sin_ref,
             ckv_ref, kr_ref, kcat_ref, ckvt_ref):
    h = _mod_norm(x_ref[...], g_ref[...], scale_ref[...], shift_ref[...]).astype(BF16)
    ckv = _rms(_dot(h, wc_ref[...])) * gl_ref[...]
    kr = _dot(h, wr_ref[...]) * cos_ref[...] + _dot(h, wrr_ref[...]) * sin_ref[...]
    ckv_ref[...] = ckv
    kr_ref[...] = kr
    kcat_ref[:, 0:KV_RANK] = ckv.astype(BF16)
    kcat_ref[:, KV_RANK:QK_LAT] = kr.astype(BF16)
    ckvt_ref[0:KV_RANK, :] = ckv.T.astype(BF16)
    ckvt_ref[KV_RANK:VT_ROWS, :] = jnp.ones((VT_ROWS - KV_RANK, ckvt_ref.shape[1]), BF16)


def _shared_kv(grp, x, kvmods, g_norm_row, w_c, w_r, w_rr, g_latent_row, cos_t, sin_t):
    tps = grp.tiles_per_seq
    return pl.pallas_call(
        _kv_body,
        out_shape=[jax.ShapeDtypeStruct((grp.rows, KV_RANK), F32),
                   jax.ShapeDtypeStruct((grp.rows, QK_ROPE), F32),
                   jax.ShapeDtypeStruct((grp.rows, QK_LAT), BF16),
                   jax.ShapeDtypeStruct((grp.rows // grp.seq, VT_ROWS, grp.seq), BF16)],
        grid=(grp.rows // grp.tm,),
        in_specs=[_row_spec(grp, D_MODEL),
                  _mod_spec(grp, kvmods, 0),
                  _mod_spec(grp, kvmods, 1),
                  _full_spec((1, D_MODEL)),
                  _full_spec(w_c.shape), _full_spec(w_r.shape), _full_spec(w_rr.shape),
                  _full_spec((1, KV_RANK)),
                  _pos_spec(grp, QK_ROPE), _pos_spec(grp, QK_ROPE)],
        out_specs=[_row_spec(grp, KV_RANK), _row_spec(grp, QK_ROPE), _row_spec(grp, QK_LAT),
                   pl.BlockSpec((None, VT_ROWS, grp.tm), lambda i: (i // tps, 0, i % tps))],
        compiler_params=_params("parallel"),
        name="shared_kv",
    )(x, kvmods, kvmods, g_norm_row, w_c, w_r, w_rr, g_latent_row, cos_t, sin_t)


def _mla_q_body(x_ref, shift_ref, scale_ref, g_ref, wdq_ref, gq_ref, wn_ref, wrt_ref, wrrt_ref, wukt_ref,
                cos_ref, sin_ref, q_ref):
    tm = x_ref.shape[0]
    h = _mod_norm(x_ref[...], g_ref[...], scale_ref[...], shift_ref[...]).astype(BF16)
    qc = (_rms(_dot(h, wdq_ref[...])) * gq_ref[...]).astype(BF16)
    qn = _dot(qc, wn_ref[...])
    qrt = (_dot_nt(wrt_ref[...], qc) * cos_ref[...] + _dot_nt(wrrt_ref[...], qc) * sin_ref[...]) * SM_SCALE
    for hd in range(MLA_HEADS):
        cols = slice(hd * tm, (hd + 1) * tm)
        qlt = _dot_nt(wukt_ref[hd], qn[:, hd * QK_NOPE:(hd + 1) * QK_NOPE].astype(BF16)) * SM_SCALE
        q_ref[0:KV_RANK, cols] = qlt.astype(q_ref.dtype)
        q_ref[KV_RANK:QK_LAT, cols] = qrt[hd * QK_ROPE:(hd + 1) * QK_ROPE, :].astype(q_ref.dtype)


def _mla_q(grp, x, mods, g_norm_row, w_dq, g_q_row, w_n, w_rt, w_rrt, w_ukt, cos_t, sin_t, layer_b):
    sub = 1
    nseq = grp.rows // grp.seq
    tps = grp.tiles_per_seq
    width = MLA_HEADS * QK_ROPE
    sel = lambda a: pl.BlockSpec((None,) + a.shape[1:], lambda i: (layer_b,) + (0,) * (a.ndim - 1))
    if grp.per_row:
        pos = pl.BlockSpec((width, grp.tm), lambda i: (0, i))
    else:
        pos = pl.BlockSpec((width, grp.tm), lambda i: (0, i % tps))
    return pl.pallas_call(
        _mla_q_body,
        out_shape=jax.ShapeDtypeStruct((nseq, tps, QK_LAT, MLA_HEADS * grp.tm), BF16),
        grid=(grp.rows // grp.tm,),
        in_specs=[_row_spec(grp, D_MODEL),
                  _mod_spec(grp, mods, sub * 3 + 0),
                  _mod_spec(grp, mods, sub * 3 + 1),
                  _full_spec((1, D_MODEL)),
                  sel(w_dq), sel(g_q_row), sel(w_n), sel(w_rt), sel(w_rrt),
                  _full_spec(w_ukt.shape),
                  pos, pos],
        out_specs=pl.BlockSpec((None, None, QK_LAT, MLA_HEADS * grp.tm), lambda i: (i // tps, i % tps, 0, 0)),
        compiler_params=_params("parallel"),
        name="mla_q",
    )(x, mods, mods, g_norm_row, w_dq, g_q_row, w_n, w_rt, w_rrt, w_ukt, cos_t, sin_t)


def _attn_body(qi_ref, kj_ref, qt_ref, k_ref, vt_ref, o_ref, m_scr, acc_scr, *, tq):
    p = pl.program_id(1)
    i = qi_ref[p]
    j = kj_ref[p]

    @pl.when(j == 0)
    def _():
        m_scr[...] = jnp.full_like(m_scr, -jnp.inf)
        acc_scr[...] = jnp.zeros_like(acc_scr)

    def update(masked):
        s = _dot(k_ref[...], qt_ref[...])
        if masked:
            kpos = lax.broadcasted_iota(jnp.int32, s.shape, 0)
            qpos = lax.broadcasted_iota(jnp.int32, s.shape, 1) & (tq - 1)
            s = jnp.where(kpos <= qpos, s, -jnp.inf)
        m_prev = m_scr[...]
        m_new = jnp.maximum(m_prev, jnp.max(s, axis=0, keepdims=True))
        alpha = jnp.exp(m_prev - m_new)
        pr = jnp.exp(s - m_new).astype(BF16)
        acc_scr[...] = alpha * acc_scr[...] + _dot(vt_ref[...], pr)
        m_scr[...] = m_new

    @pl.when(j < i)
    def _():
        update(False)

    @pl.when(j == i)
    def _():
        update(True)
        out = acc_scr[0:KV_RANK, :] / acc_scr[KV_RANK:KV_RANK + 1, :]
        for hd in range(MLA_HEADS):
            o_ref[:, hd * KV_RANK:(hd + 1) * KV_RANK] = out[:, hd * tq:(hd + 1) * tq].T.astype(o_ref.dtype)


def _attend_prompt(qt, kcat, ckvt, batch, seq, tq):
    nq = seq // tq
    pairs = [(i, j) for i in range(nq) for j in range(i + 1)]
    qi = jnp.asarray(np.array([p[0] for p in pairs], np.int32))
    kj = jnp.asarray(np.array([p[1] for p in pairs], np.int32))
    cols = MLA_HEADS * tq
    return pl.pallas_call(
        functools.partial(_attn_body, tq=tq),
        out_shape=jax.ShapeDtypeStruct((batch, seq, MLA_HEADS * KV_RANK), BF16),
        grid_spec=pltpu.PrefetchScalarGridSpec(
            num_scalar_prefetch=2,
            grid=(batch, len(pairs)),
            in_specs=[pl.BlockSpec((None, None, QK_LAT, cols), lambda b, p, qi, kj: (b, qi[p], 0, 0)),
                      pl.BlockSpec((None, tq, QK_LAT), lambda b, p, qi, kj: (b, kj[p], 0)),
                      pl.BlockSpec((None, VT_ROWS, tq), lambda b, p, qi, kj: (b, 0, kj[p]))],
            out_specs=pl.BlockSpec((None, tq, MLA_HEADS * KV_RANK), lambda b, p, qi, kj: (b, qi[p], 0)),
            scratch_shapes=[pltpu.VMEM((1, cols), F32), pltpu.VMEM((VT_ROWS, cols), F32)],
        ),
        compiler_params=_params("parallel", "arbitrary"),
        name="attend_prompt",
    )(qi, kj, qt, kcat.reshape(batch, seq, QK_LAT), ckvt)


PAGED_CHUNK_PAGES = 32
PAGED_SLOTS = 4


def _paged_body(pt_ref, q_ref, cn_ref, kn_ref, ckv_hbm, krt_hbm, o_ref, cbuf, kbuf, sems, *, n_pages):
    ch = PAGED_CHUNK_PAGES
    nc = n_pages // ch
    ahead = PAGED_SLOTS - 1
    nseq = pl.num_programs(0)
    b = pl.program_id(0)

    def copies(g, slot):
        out = []
        for pg in range(ch):
            page = pt_ref[g * ch + pg]
            dst = pl.ds(pg * PAGE_SIZE, PAGE_SIZE)
            out.append(pltpu.make_async_copy(ckv_hbm.at[page], cbuf.at[slot, dst, :], sems.at[0, slot]))
            out.append(pltpu.make_async_copy(krt_hbm.at[page], kbuf.at[slot, pg], sems.at[1, slot]))
        return out

    def start(g, slot):
        for cp in copies(g, slot):
            cp.start()

    def wait(slot):
        for cp in copies(0, slot):
            cp.wait()

    @pl.when(b == 0)
    def _():
        for c in range(ahead):
            start(c, c)

    q = q_ref[...]
    ql = q[:, 0:KV_RANK]
    qr = q[:, KV_RANK:QK_LAT]
    cn = cn_ref[...]
    kn = kn_ref[...]
    m = jnp.sum(ql * cn, axis=1, keepdims=True) + jnp.sum(qr * kn, axis=1, keepdims=True)
    l = jnp.ones_like(m)
    acc = jnp.broadcast_to(cn, (MLA_HEADS, KV_RANK))
    qlb = ql.astype(BF16)
    qrb = qr.astype(BF16)
    for c in range(nc):
        slot = c % PAGED_SLOTS
        g = b * nc + c
        nxt = (c + ahead) % PAGED_SLOTS
        if c + ahead < nc:
            start(g + ahead, nxt)
        else:
            @pl.when(b + 1 < nseq)
            def _():
                start(g + ahead, nxt)
        wait(slot)
        kc = cbuf[slot].astype(BF16)
        s_rope = [_dot(qrb, kbuf[slot, pg].astype(BF16)) for pg in range(ch)]
        s = _dot_nt(qlb, kc) + jnp.concatenate(s_rope, axis=1)
        m_new = jnp.maximum(m, jnp.max(s, axis=1, keepdims=True))
        alpha = jnp.exp(m - m_new)
        pr = jnp.exp(s - m_new)
        l = alpha * l + jnp.sum(pr, axis=1, keepdims=True)
        acc = alpha * acc + _dot(pr.astype(BF16), kc)
        m = m_new
    o_ref[...] = acc / l


def _attend_paged(q, ckv_new, krope_new, cache_ckv, cache_krope_t, page_table):
    nseq, n_pages = page_table.shape
    ch = PAGED_CHUNK_PAGES
    assert n_pages % (PAGED_SLOTS * ch) == 0
    per_seq = lambda *tail: pl.BlockSpec((None,) + tail, lambda i, pt: (i,) + (0,) * len(tail))
    return pl.pallas_call(
        functools.partial(_paged_body, n_pages=n_pages),
        out_shape=jax.ShapeDtypeStruct((nseq, MLA_HEADS, KV_RANK), F32),
        grid_spec=pltpu.PrefetchScalarGridSpec(
            num_scalar_prefetch=1,
            grid=(nseq,),
            in_specs=[per_seq(MLA_HEADS, QK_LAT), per_seq(1, KV_RANK), per_seq(1, QK_ROPE),
                      pl.BlockSpec(memory_space=pl.ANY), pl.BlockSpec(memory_space=pl.ANY)],
            out_specs=per_seq(MLA_HEADS, KV_RANK),
            scratch_shapes=[pltpu.VMEM((PAGED_SLOTS, ch * PAGE_SIZE, KV_RANK), F32),
                            pltpu.VMEM((PAGED_SLOTS, ch, QK_ROPE, PAGE_SIZE), F32),
                            pltpu.SemaphoreType.DMA((2, PAGED_SLOTS))],
        ),
        compiler_params=_params("arbitrary"),
        name="attend_paged",
    )(page_table.reshape(-1), q, ckv_new.reshape(nseq, 1, KV_RANK), krope_new.reshape(nseq, 1, QK_ROPE),
      cache_ckv, cache_krope_t)


def _attn_out_body(o_ref, wuv_ref, wo_ref, x_ref, gate_ref, out_ref):
    parts = [_dot(o_ref[:, hd * KV_RANK:(hd + 1) * KV_RANK], wuv_ref[hd]).astype(BF16)
             for hd in range(MLA_HEADS)]
    o = jnp.concatenate(parts, axis=1)
    out_ref[...] = x_ref[...] + gate_ref[...] * _dot(o, wo_ref[...])


def _attn_out(grp, o_lat, w_uv_t, w_out, layer_b, x, mods, sub):
    width = MLA_HEADS * KV_RANK
    return pl.pallas_call(
        _attn_out_body,
        out_shape=jax.ShapeDtypeStruct((grp.rows, D_MODEL), F32),
        grid=(grp.rows // grp.tm,),
        in_specs=[_row_spec(grp, width),
                  _full_spec(w_uv_t.shape),
                  pl.BlockSpec((None, MLA_HEADS * V_HEAD, D_MODEL), lambda i: (layer_b, 0, 0)),
                  _row_spec(grp, D_MODEL),
                  _mod_spec(grp, mods, sub * 3 + 2)],
        out_specs=_row_spec(grp, D_MODEL),
        compiler_params=_params("parallel"),
        name="attn_out",
    )(o_lat, w_uv_t, w_out, x, mods)


def _rope_tables(pos):
    half = QK_ROPE // 2
    inv_freq = ROPE_THETA ** (-2.0 * jnp.arange(half, dtype=F32) / QK_ROPE)
    ang = pos.astype(F32)[:, None] * inv_freq[None, :]
    cos, sin = jnp.cos(ang), jnp.sin(ang)
    return jnp.concatenate([cos, cos], axis=-1), jnp.concatenate([-sin, sin], axis=-1)


def _swap_halves(w):
    half = QK_ROPE // 2
    return jnp.concatenate([w[..., half:], w[..., :half]], axis=-1)


def kernel(x_prompt, x_sample, c_prompt, c_sample, state_hgrn, cache_ckv, cache_krope, page_table, w_ada, b_ada, g_norm, w_ffn_gu, w_ffn_down, hg_w_in, hg_lb, hg_g_out, hg_w_out, kv_w_ada, kv_b_ada, kv_g_norm, kv_w_down, kv_g_latent, kv_w_uk, kv_w_uv, q_w_down, q_g_norm, q_w_up, attn_w_out, g_final):
    batch, seq, d = x_prompt.shape
    nseq = x_sample.shape[0]
    n_b = DEPTH - N_A_LAYERS

    w_in = hg_w_in.astype(BF16)
    w_ho = hg_w_out.astype(BF16)
    w_ao = attn_w_out.astype(BF16)
    w_kc = kv_w_down[:, :KV_RANK].astype(BF16)
    w_kr = kv_w_down[:, KV_RANK:].astype(BF16)
    w_krr = _swap_halves(kv_w_down[:, KV_RANK:]).astype(BF16)
    w_dq = q_w_down.astype(BF16)
    w_qn = q_w_up[..., :QK_NOPE].reshape(n_b, Q_RANK, MLA_HEADS * QK_NOPE).astype(BF16)
    w_qr = q_w_up[..., QK_NOPE:].reshape(n_b, Q_RANK, MLA_HEADS * QK_ROPE)
    w_qrr = _swap_halves(q_w_up[..., QK_NOPE:]).reshape(n_b, Q_RANK, MLA_HEADS * QK_ROPE)
    w_qrt = w_qr.transpose(0, 2, 1).astype(BF16)
    w_qrrt = w_qrr.transpose(0, 2, 1).astype(BF16)
    w_ukt = kv_w_uk.transpose(1, 0, 2).astype(BF16)
    w_uv_t = kv_w_uv.transpose(1, 0, 2).astype(BF16)
    cache_krope_t = cache_krope.transpose(0, 2, 1)
    g_q_rows = q_g_norm.reshape(n_b, 1, Q_RANK)
    g_out_rows = hg_g_out.reshape(N_A_LAYERS, 1, D_MODEL)

    pad = 8
    c_all = jnp.concatenate([c_prompt, jnp.zeros((pad - batch, d), F32), c_sample], axis=0)
    mods_all = [_ada(c_all, w_ada, b_ada.reshape(DEPTH, 1, -1), l) for l in range(DEPTH)]
    kvm_all = _ada(c_all, kv_w_ada[None], kv_b_ada.reshape(1, 1, -1), 0)

    def trunk(grp, x, mods_l, kvmods, pos, mix, attend, stream_dtype):
        grp_half = grp._replace(tm=min(grp.tm, WIDE_TM))
        grp_q = grp._replace(tm=min(grp.tm, ATTN_TQ))
        cos_k, sin_k = _rope_tables(pos)
        cos_qt, sin_qt = jnp.tile(cos_k, (1, MLA_HEADS)).T, jnp.tile(sin_k, (1, MLA_HEADS)).T
        states = []
        ckv = krope = kcat = ckvt = None
        for l in range(DEPTH):
            mods = mods_l[l]
            gn = lambda j: g_norm[l, j].reshape(1, d)
            if l == N_A_LAYERS:
                ckv, krope, kcat, ckvt = _shared_kv(grp, x, kvmods, kv_g_norm.reshape(1, d), w_kc, w_kr, w_krr,
                                                    kv_g_latent.reshape(1, KV_RANK), cos_k, sin_k)
            x = _ffn(grp, x, mods, 0, gn(0), w_ffn_gu, w_ffn_down, l, 0)
            if l < N_A_LAYERS:
                qs, k, v, og, lg = _hgrn_in(grp_half, x, mods, gn(1), w_in, hg_lb, l, stream_dtype)
                y, s = mix(l, qs, k, v, og, lg)
                states.append(s)
                x = _proj_res(grp, y, w_ho, l, x, mods, 1)
            else:
                lb_ = l - N_A_LAYERS
                qt = _mla_q(grp_q, x, mods, gn(1), w_dq, g_q_rows, w_qn, w_qrt, w_qrrt, w_ukt, cos_qt, sin_qt,
                            lb_)
                o_lat = attend(qt, ckv, krope, kcat, ckvt)
                x = _attn_out(grp, o_lat, w_uv_t, w_ao, lb_, x, mods, 1)
            x = _ffn(grp, x, mods, 2, gn(2), w_ffn_gu, w_ffn_down, l, 1,
                     g_final=g_final.reshape(1, d) if l == DEPTH - 1 else None)
        return x, jnp.stack(states), ckv, krope

    grp_p = Group(rows=batch * seq, seq=seq, tm=PROMPT_TM, per_row=False)
    mods_p = [m[:pad].reshape(pad, 1, -1) for m in mods_all]
    kvm_p = kvm_all[:pad].reshape(pad, 1, -1)

    def mix_p(l, qs, k, v, og, lg):
        y, s = _gla_prompt(qs, k, v, og, lg, g_out_rows[l], batch, seq)
        return y.reshape(batch * seq, d), s

    def attend_p(qt, ckv, krope, kcat, ckvt):
        o = _attend_prompt(qt, kcat, ckvt, batch, seq, ATTN_TQ)
        return o.reshape(batch * seq, MLA_HEADS * KV_RANK)

    y_p, st_p, ckv_p, kr_p = trunk(grp_p, x_prompt.reshape(batch * seq, d), mods_p, kvm_p,
                                   jnp.arange(seq), mix_p, attend_p, BF16)

    assert nseq == MIN_TM
    grp_s = Group(rows=nseq, seq=nseq, tm=nseq, per_row=True)
    mods_s = [m[pad:] for m in mods_all]
    kvm_s = kvm_all[pad:]

    def mix_s(l, qs, k, v, og, lg):
        y, s = _gla_step(qs, k, v, og, lg, hg_g_out[l].reshape(HG_HEADS, HG_DV), state_hgrn, l)
        return y.reshape(nseq, d), s

    def attend_s(qt, ckv, krope, kcat, ckvt):
        qf = qt.reshape(QK_LAT, MLA_HEADS, nseq).transpose(2, 1, 0).astype(F32)
        o = _attend_paged(qf, ckv, krope, cache_ckv, cache_krope_t, page_table)
        return o.reshape(nseq, MLA_HEADS * KV_RANK).astype(BF16)

    y_s, st_s, ckv_s, kr_s = trunk(grp_s, x_sample.reshape(nseq, d), mods_s, kvm_s,
                                   jnp.full((nseq,), PAST_LEN), mix_s, attend_s, F32)

    return (y_p.reshape(batch, seq, d), y_s.reshape(nseq, 1, d),
            st_p.astype(state_hgrn.dtype), st_s.astype(state_hgrn.dtype),
            ckv_p.reshape(batch, seq, KV_RANK), kr_p.reshape(batch, seq, QK_ROPE),
            ckv_s.reshape(nseq, 1, KV_RANK), kr_s.reshape(nseq, 1, QK_ROPE))
```
